```python
import math
import jax
import jax.numpy as jnp
from jax import lax
import numpy as np

D_MODEL = 4096
BATCH = 2
SEQ = 8192
DEPTH = 2

GRID_W = 64
CTX_LEN = 256

NA_HEAD_DIM = 128
NA_HEADS = D_MODEL // 2 // NA_HEAD_DIM
NA_WIDTH = NA_HEADS * NA_HEAD_DIM
NA_WIN_ROWS = 8
NA_WIN_COLS = 16

RET_HEADS = 8
RET_V_DIM = D_MODEL // 2 // RET_HEADS
RET_QK_DIM = RET_V_DIM // 2
RET_QK_WIDTH = RET_HEADS * RET_QK_DIM
RET_V_WIDTH = RET_HEADS * RET_V_DIM
RET_CHUNK = 128

EVEN_SPLITS = (NA_WIDTH, 2 * NA_WIDTH, 3 * NA_WIDTH,
               3 * NA_WIDTH + RET_QK_WIDTH, 3 * NA_WIDTH + 2 * RET_QK_WIDTH,
               3 * NA_WIDTH + 2 * RET_QK_WIDTH + RET_V_WIDTH)
EVEN_IN_WIDTH = 3 * NA_WIDTH + 2 * RET_QK_WIDTH + 2 * RET_V_WIDTH
EVEN_OUT_WIDTH = NA_WIDTH + RET_V_WIDTH

S5_GROUP_CH = 16
S5_GROUPS = D_MODEL // S5_GROUP_CH
S5_STATE = 64
S5_CHUNK = 128
S5_DT_MIN = 1e-3
S5_DT_MAX = 1e-1

MOE_GROUPS = 4
MOE_EXPERTS_PER_GROUP = 8
MOE_EXPERTS = MOE_GROUPS * MOE_EXPERTS_PER_GROUP
MOE_TOP_K = 2
MOE_HIDDEN = D_MODEL // 8
EXPERT_BLOCK = 256

ROPE_BASE = 10000.0
LN_EPS = 1e-5
N_EVEN = (DEPTH + 1) // 2
N_ODD = DEPTH // 2
DEEPNORM_ALPHA = (2.0 * DEPTH) ** 0.25
DEEPNORM_BETA = (8.0 * DEPTH) ** -0.25

kernel_name = 'hybrid_natten_retnet_s5_hmoe_dit'


def layer_norm(x, gain, bias):
    xf = x.astype(jnp.float32)
    mu = xf.mean(-1, keepdims=True)
    var = jnp.square(xf - mu).mean(-1, keepdims=True)
    y = (xf - mu) * lax.rsqrt(var + LN_EPS) * gain.astype(jnp.float32) + bias.astype(jnp.float32)
    return y.astype(x.dtype)


def head_norm(x):
    mu = x.mean(-1, keepdims=True)
    var = jnp.square(x - mu).mean(-1, keepdims=True)
    return (x - mu) * lax.rsqrt(var + LN_EPS)


def axial_rope(x, pos_row, pos_col):
    half = x.shape[-1] // 2
    n_freq = half // 2
    freq = ROPE_BASE ** (-jnp.arange(n_freq, dtype=jnp.float32) / n_freq)

    def rotate(xp, pos):
        ang = pos.astype(jnp.float32)[:, None] * freq[None, :]
        cos = jnp.cos(ang)[None, :, None, :]
        sin = jnp.sin(ang)[None, :, None, :]
        x1, x2 = xp[..., :n_freq], xp[..., n_freq:]
        return jnp.concatenate([x1 * cos - x2 * sin, x1 * sin + x2 * cos], axis=-1)

    return jnp.concatenate([rotate(x[..., :half], pos_row), rotate(x[..., half:], pos_col)], axis=-1)


def neighborhood_attention(q, k, v, qc, kc, vc, rpb, ctx_out):
    B, L, H, hd = q.shape
    rows = L // GRID_W
    wr = min(NA_WIN_ROWS, rows)
    wc = NA_WIN_COLS
    scale = hd ** -0.5
    qg = q.reshape(B, rows, GRID_W, H, hd)
    kg = k.reshape(B, rows, GRID_W, H, hd)
    vg = v.reshape(B, rows, GRID_W, H, hd)
    col = jnp.arange(GRID_W)
    col_start = jnp.clip(col - wc // 2, 0, GRID_W - wc)
    col_ok = (col[None, :] >= col_start[:, None]) & (col[None, :] < col_start[:, None] + wc)
    col_idx = jnp.clip(col[None, :] - col[:, None] + NA_WIN_COLS - 1, 0, 2 * NA_WIN_COLS - 2)
    rpb_col = rpb[:, :, col_idx]
    n_loc = wr * GRID_W

    def row_block(r):
        rs = jnp.clip(r - wr // 2, 0, rows - wr)
        q_r = lax.dynamic_index_in_dim(qg, r, axis=1, keepdims=False)
        k_r = lax.dynamic_slice_in_dim(kg, rs, wr, axis=1)
        v_r = lax.dynamic_slice_in_dim(vg, rs, wr, axis=1)
        row_idx = rs + jnp.arange(wr) - r + (NA_WIN_ROWS - 1)
        bias = jnp.take(rpb_col, row_idx, axis=1).transpose(0, 2, 1, 3)
        s_loc = jnp.einsum('bqhd,bjkhd->bhqjk', q_r, k_r) * scale + bias[None]
        s_loc = jnp.where(col_ok[:, None, :], s_loc, -jnp.inf).reshape(B, H, GRID_W, n_loc)
        s_ctx = jnp.einsum('bqhd,bchd->bhqc', q_r, kc) * scale
        p = jax.nn.softmax(jnp.concatenate([s_loc, s_ctx], axis=-1), axis=-1)
        p_loc = p[..., :n_loc].reshape(B, H, GRID_W, wr, GRID_W)
        return (jnp.einsum('bhqjk,bjkhd->bqhd', p_loc, v_r)
                + jnp.einsum('bhqc,bchd->bqhd', p[..., n_loc:], vc))

    o = lax.map(row_block, jnp.arange(rows))
    o = jnp.moveaxis(o, 0, 1).reshape(B, L, H, hd)
    oc = None
    if ctx_out:
        pc = jax.nn.softmax(jnp.einsum('bqhd,bkhd->bhqk', qc, kc) * scale, axis=-1)
        oc = jnp.einsum('bhqk,bkhd->bqhd', pc, vc)
    return o, oc


def retention_chunkwise(q, k, v, log_gamma, s0, with_output):
    B, L, H, _ = q.shape
    dv = v.shape[-1]
    n = L // RET_CHUNK
    idx = jnp.arange(RET_CHUNK, dtype=jnp.float32)
    rel = idx[:, None] - idx[None, :]
    intra = jnp.where(rel >= 0, jnp.exp(log_gamma[:, None, None] * jnp.maximum(rel, 0.0)), 0.0)
    q_dec = jnp.exp(log_gamma[None, :] * (idx[:, None] + 1.0))[None, :, :, None]
    k_dec = jnp.exp(log_gamma[None, :] * (RET_CHUNK - 1.0 - idx[:, None]))[None, :, :, None]
    c_dec = jnp.exp(log_gamma * RET_CHUNK)[None, :, None, None]

    def blocks(z):
        return z.reshape(B, n, RET_CHUNK, H, z.shape[-1]).swapaxes(0, 1)

    def step(s, qkv):
        qi, ki, vi = qkv
        s_new = c_dec * s + jnp.einsum('bchd,bche->bhde', ki * k_dec, vi)
        if not with_output:
            return s_new, None
        scores = jnp.einsum('bqhd,bkhd->bhqk', qi, ki) * intra[None]
        o = (jnp.einsum('bhqk,bkhe->bqhe', scores, vi)
             + jnp.einsum('bqhd,bhde->bqhe', qi * q_dec, s))
        return s_new, o

    s_fin, o = lax.scan(step, s0, (blocks(q), blocks(k), blocks(v)))
    if with_output:
        o = o.swapaxes(0, 1).reshape(B, L, H, dv)
    return o, s_fin


def bidirectional_retention(q, k, v, qc, kc, vc, log_dec_f, log_dec_b, ctx_out):
    s0 = jnp.zeros((q.shape[0], RET_HEADS, RET_QK_DIM, RET_V_DIM), jnp.float32)
    lf = log_dec_f.astype(jnp.float32)
    lb = log_dec_b.astype(jnp.float32)

    def flip(t):
        return t[:, ::-1]

    oc_f, s_cf = retention_chunkwise(qc, kc, vc, lf, s0, ctx_out)
    o_f, _ = retention_chunkwise(q, k, v, lf, s_cf, True)
    oc_b, s_cb = retention_chunkwise(flip(qc), flip(kc), flip(vc), lb, s0, ctx_out)
    o_b, _ = retention_chunkwise(flip(q), flip(k), flip(v), lb, s_cb, True)
    o = o_f + flip(o_b)
    oc = oc_f + flip(oc_b) if ctx_out else None
    return o, oc


def na_retention_mixer(h, hc, w_in, w_out, rpb, log_dec_f, log_dec_b, pos_row, pos_col, ctx_out):
    z = jnp.split(h @ w_in, EVEN_SPLITS, axis=-1)
    zc = jnp.split(hc @ w_in, EVEN_SPLITS, axis=-1)

    def heads(t, n_h):
        return t.reshape(t.shape[0], t.shape[1], n_h, -1).astype(jnp.float32)

    o_na, oc_na = neighborhood_attention(
        heads(z[0], NA_HEADS), heads(z[1], NA_HEADS), heads(z[2], NA_HEADS),
        heads(zc[0], NA_HEADS), heads(zc[1], NA_HEADS), heads(zc[2], NA_HEADS),
        rpb.astype(jnp.float32), ctx_out)
    k_scale = RET_QK_DIM ** -0.5
    qb = axial_rope(heads(z[3], RET_HEADS), pos_row, pos_col)
    kb = axial_rope(heads(z[4], RET_HEADS), pos_row, pos_col) * k_scale
    o_ret, oc_ret = bidirectional_retention(
        qb, kb, heads(z[5], RET_HEADS),
        heads(zc[3], RET_HEADS), heads(zc[4], RET_HEADS) * k_scale, heads(zc[5], RET_HEADS),
        log_dec_f, log_dec_b, ctx_out)

    def merge(o_a, o_b, gate):
        b, n = o_a.shape[:2]
        y_b = head_norm(o_b).reshape(b, n, RET_V_WIDTH) * jax.nn.silu(gate.astype(jnp.float32))
        y = jnp.concatenate([o_a.reshape(b, n, NA_WIDTH), y_b], axis=-1)
        return y.astype(h.dtype) @ w_out

    out = merge(o_na, o_ret, z[6])
    out_c = merge(oc_na, oc_ret, zc[6]) if ctx_out else None
    return out, out_c


def s5_discretize(lam_re, lam_im, log_dt, b_re, b_im):
    dt = jnp.exp(log_dt)[:, None]
    logmag = lam_re * dt
    angle = lam_im * dt
    nr = jnp.exp(logmag) * jnp.cos(angle) - 1.0
    ni = jnp.exp(logmag) * jnp.sin(angle)
    den = lam_re * lam_re + lam_im * lam_im
    fr = ((nr * lam_re + ni * lam_im) / den)[..., None]
    fi = ((ni * lam_re - nr * lam_im) / den)[..., None]
    bb_re = fr * b_re - fi * b_im
    bb_im = fr * b_im + fi * b_re
    return logmag, angle, bb_re, bb_im


def complex_affine_combine(e1, e2):
    a1r, a1i, b1r, b1i = e1
    a2r, a2i, b2r, b2i = e2
    return (a2r * a1r - a2i * a1i, a2r * a1i + a2i * a1r,
            a2r * b1r - a2i * b1i + b2r, a2r * b1i + a2i * b1r + b2i)


def s5_scan(u, logmag, angle, bb_re, bb_im, c_re, c_im, x0_re, x0_im, with_output):
    B, L, G, CH = u.shape
    n = L // S5_CHUNK
    t1 = jnp.arange(1, S5_CHUNK + 1, dtype=jnp.float32)[:, None, None]
    mag = jnp.exp(logmag * t1)
    pw_re = (mag * jnp.cos(angle * t1))[None]
    pw_im = (mag * jnp.sin(angle * t1))[None]
    shape = (B, S5_CHUNK, G, S5_STATE)
    a_re = jnp.broadcast_to(jnp.exp(logmag) * jnp.cos(angle), shape)
    a_im = jnp.broadcast_to(jnp.exp(logmag) * jnp.sin(angle), shape)

    def step(carry, ub):
        xr0, xi0 = carry
        bu_re = jnp.einsum('bcgh,gph->bcgp', ub, bb_re)
        bu_im = jnp.einsum('bcgh,gph->bcgp', ub, bb_im)
        _, _, xr, xi = lax.associative_scan(complex_affine_combine, (a_re, a_im, bu_re, bu_im), axis=1)
        xr = xr + pw_re * xr0[:, None] - pw_im * xi0[:, None]
        xi = xi + pw_re * xi0[:, None] + pw_im * xr0[:, None]
        y = None
        if with_output:
            y = jnp.einsum('bcgp,ghp->bcgh', xr, c_re) - jnp.einsum('bcgp,ghp->bcgh', xi, c_im)
        return (xr[:, -1], xi[:, -1]), y

    u_blocks = u.reshape(B, n, S5_CHUNK, G, CH).swapaxes(0, 1)
    (xr, xi), y = lax.scan(step, (x0_re, x0_im), u_blocks)
    if with_output:
        y = y.swapaxes(0, 1).reshape(B, L, G, CH)
    return y, xr, xi


def s5_mixer(h, hc, w_in, lam_re, lam_im, log_dt, b_re, b_im, c_re, c_im, d_skip, w_glu, w_out, ctx_out):
    B, L, D = h.shape
    u = (h @ w_in).astype(jnp.float32).reshape(B, L, S5_GROUPS, S5_GROUP_CH)
    uc = (hc @ w_in).astype(jnp.float32).reshape(hc.shape[0], hc.shape[1], S5_GROUPS, S5_GROUP_CH)
    x0 = jnp.zeros((B, S5_GROUPS, S5_STATE), jnp.float32)
    b_re = b_re.astype(jnp.float32)
    b_im = b_im.astype(jnp.float32)
    dsk = d_skip.astype(jnp.float32).reshape(S5_GROUPS, S5_GROUP_CH)
    y = dsk * u
    yc = dsk * uc if ctx_out else None
    for direction in range(2):
        logmag, angle, bb_re, bb_im = s5_discretize(
            lam_re[direction].astype(jnp.float32), lam_im[direction].astype(jnp.float32),
            log_dt[direction].astype(jnp.float32), b_re, b_im)
        cr = c_re[direction].astype(jnp.float32)
        ci = c_im[direction].astype(jnp.float32)
        order = (lambda t: t[:, ::-1]) if direction == 1 else (lambda t: t)
        yc_d, xr, xi = s5_scan(order(uc), logmag, angle, bb_re, bb_im, cr, ci, x0, x0, ctx_out)
        y_d, _, _ = s5_scan(order(u), logmag, angle, bb_re, bb_im, cr, ci, xr, xi, True)
        y = y + order(y_d)
        if ctx_out:
            yc = yc + order(yc_d)

    def readout(yy):
        g = jax.nn.gelu(yy.reshape(yy.shape[0], yy.shape[1], D)).astype(h.dtype)
        return (g * jax.nn.sigmoid(g @ w_glu)) @ w_out

    return readout(y), (readout(yc) if ctx_out else None)


def routed_experts(h, eid, wts, w_gate, w_up, w_down):
    n_tok, d = h.shape
    n_asg = n_tok * MOE_TOP_K
    flat_e = eid.reshape(-1)
    order = jnp.argsort(flat_e)
    e_sorted = flat_e[order]
    counts = jnp.bincount(flat_e, length=MOE_EXPERTS)
    padded = (counts + EXPERT_BLOCK - 1) // EXPERT_BLOCK * EXPERT_BLOCK
    pad_end = jnp.cumsum(padded)
    pad_start = pad_end - padded
    start = jnp.cumsum(counts) - counts
    dest = pad_start[e_sorted] + jnp.arange(n_asg) - start[e_sorted]
    n_blocks = (n_asg + MOE_EXPERTS * (EXPERT_BLOCK - 1)) // EXPERT_BLOCK + 1
    n_rows = n_blocks * EXPERT_BLOCK
    row_tok = jnp.full((n_rows,), n_tok, jnp.int32).at[dest].set((order // MOE_TOP_K).astype(jnp.int32))
    row_w = jnp.zeros((n_rows,), jnp.float32).at[dest].set(wts.reshape(-1)[order])
    blk_e = jnp.minimum(jnp.searchsorted(pad_end, jnp.arange(n_blocks) * EXPERT_BLOCK, side='right'),
                        MOE_EXPERTS - 1)
    h_pad = jnp.concatenate([h, jnp.zeros((1, d), h.dtype)], axis=0)
    xb = h_pad[row_tok].reshape(n_blocks, EXPERT_BLOCK, d)

    def expert_block(args):
        xblk, e = args
        return (jax.nn.silu(xblk @ w_gate[e]) * (xblk @ w_up[e])) @ w_down[e]

    yb = lax.map(expert_block, (xb, blk_e)).reshape(n_rows, d)
    out = jax.ops.segment_sum(yb.astype(jnp.float32) * row_w[:, None], row_tok, num_segments=n_tok + 1)
    return out[:n_tok].astype(h.dtype)


def hier_moe(h, w_group, b_group, w_expert, b_expert, w_gate, w_up, w_down):
    n_tok = h.shape[0]
    g_prob = jax.nn.softmax((h @ w_group).astype(jnp.float32) + b_group.astype(jnp.float32), axis=-1)
    g_w, g_idx = lax.top_k(g_prob, 1)
    e_logits = ((h @ w_expert).astype(jnp.float32) + b_expert.astype(jnp.float32)).reshape(
        n_tok, MOE_GROUPS, MOE_EXPERTS_PER_GROUP)
    e_in = jnp.take_along_axis(e_logits, g_idx[:, :, None], axis=1)[:, 0]
    top_l, top_i = lax.top_k(e_in, MOE_TOP_K)
    wts = g_w * jax.nn.softmax(top_l, axis=-1)
    eid = g_idx * MOE_EXPERTS_PER_GROUP + top_i
    return routed_experts(h, eid, wts, w_gate, w_up, w_down)


def setup_inputs(seed: int = 0) -> dict:
    key = jax.random.key(seed)
    ks = list(jax.random.split(key, 40))

    def nrm(shape, scale):
        return scale * jax.random.normal(ks.pop(), shape, jnp.float32)

    d = D_MODEL
    base_decay = jnp.log(1.0 - 2.0 ** (-5.0 - jnp.arange(RET_HEADS, dtype=jnp.float32)))
    return {
        'x': nrm((BATCH, SEQ, d), 1.0),
        'c': nrm((BATCH, d), 1.0),
        'ctx': nrm((BATCH, CTX_LEN, d), 1.0),
        'c_ctx': nrm((d,), 1.0),
        'mod_w': nrm((DEPTH, d, 6 * d), d ** -0.5),
        'mod_b': nrm((DEPTH, 6 * d), 0.02),
        'ln_g': 1.0 + nrm((DEPTH, 2, d), 0.02),
        'ln_b': nrm((DEPTH, 2, d), 0.02),
        'mix_w_in': nrm((N_EVEN, d, EVEN_IN_WIDTH), d ** -0.5),
        'mix_w_out': nrm((N_EVEN, EVEN_OUT_WIDTH, d), DEEPNORM_BETA * EVEN_OUT_WIDTH ** -0.5),
        'na_rpb': nrm((N_EVEN, NA_HEADS, 2 * NA_WIN_ROWS - 1, 2 * NA_WIN_COLS - 1), 0.02),
        'ret_log_decay_fwd': base_decay * (1.0 + nrm((N_EVEN, RET_HEADS), 0.05)),
        'ret_log_decay_bwd': base_decay * (1.0 + nrm((N_EVEN, RET_HEADS), 0.05)),
        's5_w_in': nrm((N_ODD, d, d), d ** -0.5),
        's5_lam_re': -0.5 + nrm((N_ODD, 2, S5_GROUPS, S5_STATE), 0.01),
        's5_lam_im': jnp.pi * jnp.arange(S5_STATE, dtype=jnp.float32) + nrm((N_ODD, 2, S5_GROUPS, S5_STATE), 0.01),
        's5_log_dt': jax.random.uniform(ks.pop(), (N_ODD, 2, S5_GROUPS), jnp.float32,
                                        math.log(S5_DT_MIN), math.log(S5_DT_MAX)),
        's5_b_re': nrm((N_ODD, S5_GROUPS, S5_STATE, S5_GROUP_CH), (2 * S5_GROUP_CH) ** -0.5),
        's5_b_im': nrm((N_ODD, S5_GROUPS, S5_STATE, S5_GROUP_CH), (2 * S5_GROUP_CH) ** -0.5),
        's5_c_re': nrm((N_ODD, 2, S5_GROUPS, S5_GROUP_CH, S5_STATE), S5_STATE ** -0.5),
        's5_c_im': nrm((N_ODD, 2, S5_GROUPS, S5_GROUP_CH, S5_STATE), S5_STATE ** -0.5),
        's5_d': nrm((N_ODD, d), 1.0),
        's5_w_glu': nrm((N_ODD, d, d), d ** -0.5),
        's5_w_out': nrm((N_ODD, d, d), DEEPNORM_BETA * d ** -0.5),
        'moe_w_group': nrm((DEPTH, d, MOE_GROUPS), d ** -0.5),
        'moe_b_group': nrm((DEPTH, MOE_GROUPS), 0.01),
        'moe_w_expert': nrm((DEPTH, d, MOE_EXPERTS), d ** -0.5),
        'moe_b_expert': nrm((DEPTH, MOE_EXPERTS), 0.01),
        'moe_w_gate': nrm((DEPTH, MOE_EXPERTS, d, MOE_HIDDEN), d ** -0.5),
        'moe_w_up': nrm((DEPTH, MOE_EXPERTS, d, MOE_HIDDEN), d ** -0.5),
        'moe_w_down': nrm((DEPTH, MOE_EXPERTS, MOE_HIDDEN, d), DEEPNORM_BETA * MOE_HIDDEN ** -0.5),
    }


def reference(x, c, ctx, c_ctx, mod_w, mod_b, ln_g, ln_b, mix_w_in, mix_w_out, na_rpb,
              ret_log_decay_fwd, ret_log_decay_bwd, s5_w_in, s5_lam_re, s5_lam_im, s5_log_dt,
              s5_b_re, s5_b_im, s5_c_re, s5_c_im, s5_d, s5_w_glu, s5_w_out,
              moe_w_group, moe_b_group, moe_w_expert, moe_b_expert, moe_w_gate, moe_w_up, moe_w_down):
    B, L, D = x.shape
    t = jnp.arange(L)
    pos_row = t // GRID_W
    pos_col = t % GRID_W
    silu_c = jax.nn.silu(c)[:, None, :]
    silu_cc = jax.nn.silu(c_ctx)[None, None, :]
    xc = ctx
    for layer in range(DEPTH):
        last = layer == DEPTH - 1
        ctx_out = not last
        m = jnp.split(silu_c @ mod_w[layer] + mod_b[layer], 6, axis=-1)
        mc = jnp.split(silu_cc @ mod_w[layer] + mod_b[layer], 6, axis=-1)
        h = x * (1.0 + m[1]) + m[0]
        hc = xc * (1.0 + mc[1]) + mc[0]
        j = layer // 2
        if layer % 2 == 0:
            o, oc = na_retention_mixer(h, hc, mix_w_in[j], mix_w_out[j], na_rpb[j],
                                       ret_log_decay_fwd[j], ret_log_decay_bwd[j],
                                       pos_row, pos_col, ctx_out)
        else:
            o, oc = s5_mixer(h, hc, s5_w_in[j], s5_lam_re[j], s5_lam_im[j], s5_log_dt[j],
                             s5_b_re[j], s5_b_im[j], s5_c_re[j], s5_c_im[j], s5_d[j],
                             s5_w_glu[j], s5_w_out[j], ctx_out)
        x = layer_norm(DEEPNORM_ALPHA * x + m[2] * o, ln_g[layer, 0], ln_b[layer, 0])
        h = x * (1.0 + m[4]) + m[3]
        moe_w = (moe_w_group[layer], moe_b_group[layer], moe_w_expert[layer], moe_b_expert[layer],
                 moe_w_gate[layer], moe_w_up[layer], moe_w_down[layer])
        if last:
            f = hier_moe(h.reshape(-1, D), *moe_w).reshape(B, L, D)
        else:
            xc = layer_norm(DEEPNORM_ALPHA * xc + mc[2] * oc, ln_g[layer, 0], ln_b[layer, 0])
            hc = xc * (1.0 + mc[4]) + mc[3]
            n_lat = B * L
            f_all = hier_moe(jnp.concatenate([h.reshape(-1, D), hc.reshape(-1, D)], axis=0), *moe_w)
            f = f_all[:n_lat].reshape(B, L, D)
            fc = f_all[n_lat:].reshape(xc.shape)
            xc = layer_norm(DEEPNORM_ALPHA * xc + mc[5] * fc, ln_g[layer, 1], ln_b[layer, 1])
        x = layer_norm(DEEPNORM_ALPHA * x + m[5] * f, ln_g[layer, 1], ln_b[layer, 1])
    return x
```

```python
import functools
import math

import jax
import jax.numpy as jnp
from jax import lax
from jax.experimental import pallas as pl
from jax.experimental.pallas import tpu as pltpu

D_MODEL = 4096
DEPTH = 2
GRID_W = 64
NA_HEAD_DIM = 128
NA_HEADS = D_MODEL // 2 // NA_HEAD_DIM
NA_WIDTH = NA_HEADS * NA_HEAD_DIM
NA_WIN_ROWS = 8
NA_WIN_COLS = 16
RET_HEADS = 8
RET_V_DIM = D_MODEL // 2 // RET_HEADS
RET_QK_DIM = RET_V_DIM // 2
RET_QK_WIDTH = RET_HEADS * RET_QK_DIM
RET_V_WIDTH = RET_HEADS * RET_V_DIM
RET_CHUNK = 128
EVEN_SPLITS = (NA_WIDTH, 2 * NA_WIDTH, 3 * NA_WIDTH,
               3 * NA_WIDTH + RET_QK_WIDTH, 3 * NA_WIDTH + 2 * RET_QK_WIDTH,
               3 * NA_WIDTH + 2 * RET_QK_WIDTH + RET_V_WIDTH)
S5_GROUP_CH = 16
S5_GROUPS = D_MODEL // S5_GROUP_CH
S5_STATE = 64
S5_CHUNK = 128
MOE_GROUPS = 4
MOE_EXPERTS_PER_GROUP = 8
MOE_EXPERTS = MOE_GROUPS * MOE_EXPERTS_PER_GROUP
MOE_TOP_K = 2
EXPERT_BLOCK = 256
ROPE_BASE = 10000.0
LN_EPS = 1e-5
DEEPNORM_ALPHA = (2.0 * DEPTH) ** 0.25

VMEM_LIMIT_BYTES = 48 * 1024 * 1024
MM_TILE_M = 512
MM_TILE_N = 512


def _mm_body(x_ref, w_ref, o_ref):
    o_ref[...] = jnp.dot(x_ref[...], w_ref[...],
                         preferred_element_type=jnp.float32).astype(o_ref.dtype)


def _mm(x, w, out_dtype=jnp.float32):
    m, k = x.shape
    n = w.shape[1]
    tm = MM_TILE_M if m % MM_TILE_M == 0 else m
    tn = MM_TILE_N if n % MM_TILE_N == 0 else n
    return pl.pallas_call(
        _mm_body,
        grid=(m // tm, n // tn),
        in_specs=[pl.BlockSpec((tm, k), lambda i, j: (i, 0)),
                  pl.BlockSpec((k, tn), lambda i, j: (0, j))],
        out_specs=pl.BlockSpec((tm, tn), lambda i, j: (i, j)),
        out_shape=jax.ShapeDtypeStruct((m, n), out_dtype),
        compiler_params=pltpu.CompilerParams(
            dimension_semantics=("arbitrary", "arbitrary"),
            vmem_limit_bytes=VMEM_LIMIT_BYTES),
        name="dense_mm",
    )(x.astype(jnp.bfloat16), w.astype(jnp.bfloat16))


def _mm3(x, w, out_dtype=jnp.float32):
    lead = x.shape[:-1]
    return _mm(x.reshape(-1, x.shape[-1]), w, out_dtype).reshape(*lead, w.shape[1])


def layer_norm(x, gain, bias):
    mu = x.mean(-1, keepdims=True)
    var = jnp.square(x - mu).mean(-1, keepdims=True)
    return (x - mu) * lax.rsqrt(var + LN_EPS) * gain + bias


def head_norm(x):
    mu = x.mean(-1, keepdims=True)
    var = jnp.square(x - mu).mean(-1, keepdims=True)
    return (x - mu) * lax.rsqrt(var + LN_EPS)


def axial_rope(x, pos_row, pos_col):
    half = x.shape[-1] // 2
    n_freq = half // 2
    freq = ROPE_BASE ** (-jnp.arange(n_freq, dtype=jnp.float32) / n_freq)

    def rotate(xp, pos):
        ang = pos.astype(jnp.float32)[:, None] * freq[None, :]
        cos = jnp.cos(ang)[None, :, None, :]
        sin = jnp.sin(ang)[None, :, None, :]
        x1, x2 = xp[..., :n_freq], xp[..., n_freq:]
        return jnp.concatenate([x1 * cos - x2 * sin, x1 * sin + x2 * cos], axis=-1)

    return jnp.concatenate([rotate(x[..., :half], pos_row), rotate(x[..., half:], pos_col)], axis=-1)


def neighborhood_attention(q, k, v, qc, kc, vc, rpb, ctx_out):
    B, L, H, hd = q.shape
    rows = L // GRID_W
    wr = min(NA_WIN_ROWS, rows)
    wc = NA_WIN_COLS
    scale = hd ** -0.5
    qg = q.reshape(B, rows, GRID_W, H, hd)
    kg = k.reshape(B, rows, GRID_W, H, hd)
    vg = v.reshape(B, rows, GRID_W, H, hd)
    col = jnp.arange(GRID_W)
    col_start = jnp.clip(col - wc // 2, 0, GRID_W - wc)
    col_ok = (col[None, :] >= col_start[:, None]) & (col[None, :] < col_start[:, None] + wc)
    col_idx = jnp.clip(col[None, :] - col[:, None] + NA_WIN_COLS - 1, 0, 2 * NA_WIN_COLS - 2)
    rpb_col = rpb[:, :, col_idx]
    n_loc = wr * GRID_W

    def row_block(r):
        rs = jnp.clip(r - wr // 2, 0, rows - wr)
        q_r = lax.dynamic_index_in_dim(qg, r, axis=1, keepdims=False)
        k_r = lax.dynamic_slice_in_dim(kg, rs, wr, axis=1)
        v_r = lax.dynamic_slice_in_dim(vg, rs, wr, axis=1)
        row_idx = rs + jnp.arange(wr) - r + (NA_WIN_ROWS - 1)
        bias = jnp.take(rpb_col, row_idx, axis=1).transpose(0, 2, 1, 3)
        s_loc = jnp.einsum('bqhd,bjkhd->bhqjk', q_r, k_r) * scale + bias[None]
        s_loc = jnp.where(col_ok[:, None, :], s_loc, -jnp.inf).reshape(B, H, GRID_W, n_loc)
        s_ctx = jnp.einsum('bqhd,bchd->bhqc', q_r, kc) * scale
        p = jax.nn.softmax(jnp.concatenate([s_loc, s_ctx], axis=-1), axis=-1)
        p_loc = p[..., :n_loc].reshape(B, H, GRID_W, wr, GRID_W)
        return (jnp.einsum('bhqjk,bjkhd->bqhd', p_loc, v_r)
                + jnp.einsum('bhqc,bchd->bqhd', p[..., n_loc:], vc))

    o = lax.map(row_block, jnp.arange(rows))
    o = jnp.moveaxis(o, 0, 1).reshape(B, L, H, hd)
    oc = None
    if ctx_out:
        pc = jax.nn.softmax(jnp.einsum('bqhd,bkhd->bhqk', qc, kc) * scale, axis=-1)
        oc = jnp.einsum('bhqk,bkhd->bqhd', pc, vc)
    return o, oc


def retention_chunkwise(q, k, v, log_gamma, s0, with_output):
    B, L, H, _ = q.shape
    dv = v.shape[-1]
    n = L // RET_CHUNK
    idx = jnp.arange(RET_CHUNK, dtype=jnp.float32)
    rel = idx[:, None] - idx[None, :]
    intra = jnp.where(rel >= 0, jnp.exp(log_gamma[:, None, None] * jnp.maximum(rel, 0.0)), 0.0)
    q_dec = jnp.exp(log_gamma[None, :] * (idx[:, None] + 1.0))[None, :, :, None]
    k_dec = jnp.exp(log_gamma[None, :] * (RET_CHUNK - 1.0 - idx[:, None]))[None, :, :, None]
    c_dec = jnp.exp(log_gamma * RET_CHUNK)[None, :, None, None]

    def blocks(z):
        return z.reshape(B, n, RET_CHUNK, H, z.shape[-1]).swapaxes(0, 1)

    def step(s, qkv):
        qi, ki, vi = qkv
        s_new = c_dec * s + jnp.einsum('bchd,bche->bhde', ki * k_dec, vi)
        if not with_output:
            return s_new, None
        scores = jnp.einsum('bqhd,bkhd->bhqk', qi, ki) * intra[None]
        o = (jnp.einsum('bhqk,bkhe->bqhe', scores, vi)
             + jnp.einsum('bqhd,bhde->bqhe', qi * q_dec, s))
        return s_new, o

    s_fin, o = lax.scan(step, s0, (blocks(q), blocks(k), blocks(v)))
    if with_output:
        o = o.swapaxes(0, 1).reshape(B, L, H, dv)
    return o, s_fin


def bidirectional_retention(q, k, v, qc, kc, vc, log_dec_f, log_dec_b, ctx_out):
    s0 = jnp.zeros((q.shape[0], RET_HEADS, RET_QK_DIM, RET_V_DIM), jnp.float32)
    lf = log_dec_f.astype(jnp.float32)
    lb = log_dec_b.astype(jnp.float32)

    def flip(t):
        return t[:, ::-1]

    oc_f, s_cf = retention_chunkwise(qc, kc, vc, lf, s0, ctx_out)
    o_f, _ = retention_chunkwise(q, k, v, lf, s_cf, True)
    oc_b, s_cb = retention_chunkwise(flip(qc), flip(kc), flip(vc), lb, s0, ctx_out)
    o_b, _ = retention_chunkwise(flip(q), flip(k), flip(v), lb, s_cb, True)
    o = o_f + flip(o_b)
    oc = oc_f + flip(oc_b) if ctx_out else None
    return o, oc


def na_retention_mixer(h, hc, w_in, w_out, rpb, log_dec_f, log_dec_b, pos_row, pos_col, ctx_out):
    z = jnp.split(_mm3(h, w_in), EVEN_SPLITS, axis=-1)
    zc = jnp.split(_mm3(hc, w_in), EVEN_SPLITS, axis=-1)

    def heads(t, n_h):
        return t.reshape(t.shape[0], t.shape[1], n_h, -1).astype(jnp.float32)

    o_na, oc_na = neighborhood_attention(
        heads(z[0], NA_HEADS), heads(z[1], NA_HEADS), heads(z[2], NA_HEADS),
        heads(zc[0], NA_HEADS), heads(zc[1], NA_HEADS), heads(zc[2], NA_HEADS),
        rpb.astype(jnp.float32), ctx_out)
    k_scale = RET_QK_DIM ** -0.5
    qb = axial_rope(heads(z[3], RET_HEADS), pos_row, pos_col)
    kb = axial_rope(heads(z[4], RET_HEADS), pos_row, pos_col) * k_scale
    o_ret, oc_ret = bidirectional_retention(
        qb, kb, heads(z[5], RET_HEADS),
        heads(zc[3], RET_HEADS), heads(zc[4], RET_HEADS) * k_scale, heads(zc[5], RET_HEADS),
        log_dec_f, log_dec_b, ctx_out)

    def merge(o_a, o_b, gate):
        b, n = o_a.shape[:2]
        y_b = head_norm(o_b).reshape(b, n, RET_V_WIDTH) * jax.nn.silu(gate.astype(jnp.float32))
        y = jnp.concatenate([o_a.reshape(b, n, NA_WIDTH), y_b], axis=-1)
        return _mm3(y, w_out)

    out = merge(o_na, o_ret, z[6])
    out_c = merge(oc_na, oc_ret, zc[6]) if ctx_out else None
    return out, out_c


def s5_discretize(lam_re, lam_im, log_dt, b_re, b_im):
    dt = jnp.exp(log_dt)[:, None]
    logmag = lam_re * dt
    angle = lam_im * dt
    nr = jnp.exp(logmag) * jnp.cos(angle) - 1.0
    ni = jnp.exp(logmag) * jnp.sin(angle)
    den = lam_re * lam_re + lam_im * lam_im
    fr = ((nr * lam_re + ni * lam_im) / den)[..., None]
    fi = ((ni * lam_re - nr * lam_im) / den)[..., None]
    bb_re = fr * b_re - fi * b_im
    bb_im = fr * b_im + fi * b_re
    return logmag, angle, bb_re, bb_im


def complex_affine_combine(e1, e2):
    a1r, a1i, b1r, b1i = e1
    a2r, a2i, b2r, b2i = e2
    return (a2r * a1r - a2i * a1i, a2r * a1i + a2i * a1r,
            a2r * b1r - a2i * b1i + b2r, a2r * b1i + a2i * b1r + b2i)


def s5_scan(u, logmag, angle, bb_re, bb_im, c_re, c_im, x0_re, x0_im, with_output):
    B, L, G, CH = u.shape
    n = L // S5_CHUNK
    t1 = jnp.arange(1, S5_CHUNK + 1, dtype=jnp.float32)[:, None, None]
    mag = jnp.exp(logmag * t1)
    pw_re = (mag * jnp.cos(angle * t1))[None]
    pw_im = (mag * jnp.sin(angle * t1))[None]
    shape = (B, S5_CHUNK, G, S5_STATE)
    a_re = jnp.broadcast_to(jnp.exp(logmag) * jnp.cos(angle), shape)
    a_im = jnp.broadcast_to(jnp.exp(logmag) * jnp.sin(angle), shape)

    def step(carry, ub):
        xr0, xi0 = carry
        bu_re = jnp.einsum('bcgh,gph->bcgp', ub, bb_re)
        bu_im = jnp.einsum('bcgh,gph->bcgp', ub, bb_im)
        _, _, xr, xi = lax.associative_scan(complex_affine_combine, (a_re, a_im, bu_re, bu_im), axis=1)
        xr = xr + pw_re * xr0[:, None] - pw_im * xi0[:, None]
        xi = xi + pw_re * xi0[:, None] + pw_im * xr0[:, None]
        y = None
        if with_output:
            y = jnp.einsum('bcgp,ghp->bcgh', xr, c_re) - jnp.einsum('bcgp,ghp->bcgh', xi, c_im)
        return (xr[:, -1], xi[:, -1]), y

    u_blocks = u.reshape(B, n, S5_CHUNK, G, CH).swapaxes(0, 1)
    (xr, xi), y = lax.scan(step, (x0_re, x0_im), u_blocks)
    if with_output:
        y = y.swapaxes(0, 1).reshape(B, L, G, CH)
    return y, xr, xi


def s5_mixer(h, hc, w_in, lam_re, lam_im, log_dt, b_re, b_im, c_re, c_im, d_skip, w_glu, w_out, ctx_out):
    B, L, D = h.shape
    u = _mm3(h, w_in).reshape(B, L, S5_GROUPS, S5_GROUP_CH)
    uc = _mm3(hc, w_in).reshape(hc.shape[0], hc.shape[1], S5_GROUPS, S5_GROUP_CH)
    x0 = jnp.zeros((B, S5_GROUPS, S5_STATE), jnp.float32)
    dsk = d_skip.reshape(S5_GROUPS, S5_GROUP_CH)
    y = dsk * u
    yc = dsk * uc if ctx_out else None
    for direction in range(2):
        logmag, angle, bb_re, bb_im = s5_discretize(
            lam_re[direction], lam_im[direction], log_dt[direction], b_re, b_im)
        cr = c_re[direction]
        ci = c_im[direction]
        order = (lambda t: t[:, ::-1]) if direction == 1 else (lambda t: t)
        yc_d, xr, xi = s5_scan(order(uc), logmag, angle, bb_re, bb_im, cr, ci, x0, x0, ctx_out)
        y_d, _, _ = s5_scan(order(u), logmag, angle, bb_re, bb_im, cr, ci, xr, xi, True)
        y = y + order(y_d)
        if ctx_out:
            yc = yc + order(yc_d)

    def readout(yy):
        g = jax.nn.gelu(yy.reshape(yy.shape[0], yy.shape[1], D))
        return _mm3(g * jax.nn.sigmoid(_mm3(g, w_glu)), w_out)

    return readout(y), (readout(yc) if ctx_out else None)


def routed_experts(h, eid, wts, w_gate, w_up, w_down):
    n_tok, d = h.shape
    n_asg = n_tok * MOE_TOP_K
    flat_e = eid.reshape(-1)
    order = jnp.argsort(flat_e)
    e_sorted = flat_e[order]
    counts = jnp.bincount(flat_e, length=MOE_EXPERTS)
    padded = (counts + EXPERT_BLOCK - 1) // EXPERT_BLOCK * EXPERT_BLOCK
    pad_end = jnp.cumsum(padded)
    pad_start = pad_end - padded
    start = jnp.cumsum(counts) - counts
    dest = pad_start[e_sorted] + jnp.arange(n_asg) - start[e_sorted]
    n_blocks = (n_asg + MOE_EXPERTS * (EXPERT_BLOCK - 1)) // EXPERT_BLOCK + 1
    n_rows = n_blocks * EXPERT_BLOCK
    row_tok = jnp.full((n_rows,), n_tok, jnp.int32).at[dest].set((order // MOE_TOP_K).astype(jnp.int32))
    row_w = jnp.zeros((n_rows,), jnp.float32).at[dest].set(wts.reshape(-1)[order])
    blk_e = jnp.minimum(jnp.searchsorted(pad_end, jnp.arange(n_blocks) * EXPERT_BLOCK, side='right'),
                        MOE_EXPERTS - 1)
    h_pad = jnp.concatenate([h, jnp.zeros((1, d), h.dtype)], axis=0)
    xb = h_pad[row_tok].reshape(n_blocks, EXPERT_BLOCK, d)

    def expert_block(args):
        xblk, e = args
        return (jax.nn.silu(xblk @ w_gate[e]) * (xblk @ w_up[e])) @ w_down[e]

    yb = lax.map(expert_block, (xb, blk_e)).reshape(n_rows, d)
    out = jax.ops.segment_sum(yb.astype(jnp.float32) * row_w[:, None], row_tok, num_segments=n_tok + 1)
    return out[:n_tok].astype(h.dtype)


def hier_moe(h, w_group, b_group, w_expert, b_expert, w_gate, w_up, w_down):
    n_tok = h.shape[0]
    g_prob = jax.nn.softmax((h @ w_group) + b_group, axis=-1)
    g_w, g_idx = lax.top_k(g_prob, 1)
    e_logits = ((h @ w_expert) + b_expert).reshape(n_tok, MOE_GROUPS, MOE_EXPERTS_PER_GROUP)
    e_in = jnp.take_along_axis(e_logits, g_idx[:, :, None], axis=1)[:, 0]
    top_l, top_i = lax.top_k(e_in, MOE_TOP_K)
    wts = g_w * jax.nn.softmax(top_l, axis=-1)
    eid = g_idx * MOE_EXPERTS_PER_GROUP + top_i
    return routed_experts(h, eid, wts, w_gate, w_up, w_down)


def kernel(x, c, ctx, c_ctx, mod_w, mod_b, ln_g, ln_b, mix_w_in, mix_w_out, na_rpb,
           ret_log_decay_fwd, ret_log_decay_bwd, s5_w_in, s5_lam_re, s5_lam_im, s5_log_dt,
           s5_b_re, s5_b_im, s5_c_re, s5_c_im, s5_d, s5_w_glu, s5_w_out,
           moe_w_group, moe_b_group, moe_w_expert, moe_b_expert, moe_w_gate, moe_w_up, moe_w_down):
    B, L, D = x.shape
    t = jnp.arange(L)
    pos_row = t // GRID_W
    pos_col = t % GRID_W
    silu_c = jax.nn.silu(c)[:, None, :]
    silu_cc = jax.nn.silu(c_ctx)[None, None, :]
    xc = ctx
    for layer in range(DEPTH):
        last = layer == DEPTH - 1
        ctx_out = not last
        m = jnp.split(silu_c @ mod_w[layer] + mod_b[layer], 6, axis=-1)
        mc = jnp.split(silu_cc @ mod_w[layer] + mod_b[layer], 6, axis=-1)
        h = x * (1.0 + m[1]) + m[0]
        hc = xc * (1.0 + mc[1]) + mc[0]
        j = layer // 2
        if layer % 2 == 0:
            o, oc = na_retention_mixer(h, hc, mix_w_in[j], mix_w_out[j], na_rpb[j],
                                       ret_log_decay_fwd[j], ret_log_decay_bwd[j],
                                       pos_row, pos_col, ctx_out)
        else:
            o, oc = s5_mixer(h, hc, s5_w_in[j], s5_lam_re[j], s5_lam_im[j], s5_log_dt[j],
                             s5_b_re[j], s5_b_im[j], s5_c_re[j], s5_c_im[j], s5_d[j],
                             s5_w_glu[j], s5_w_out[j], ctx_out)
        x = layer_norm(DEEPNORM_ALPHA * x + m[2] * o, ln_g[layer, 0], ln_b[layer, 0])
        h = x * (1.0 + m[4]) + m[3]
        moe_w = (moe_w_group[layer], moe_b_group[layer], moe_w_expert[layer], moe_b_expert[layer],
                 moe_w_gate[layer], moe_w_up[layer], moe_w_down[layer])
        if last:
            f = hier_moe(h.reshape(-1, D), *moe_w).reshape(B, L, D)
        else:
            xc = layer_norm(DEEPNORM_ALPHA * xc + mc[2] * oc, ln_g[layer, 0], ln_b[layer, 0])
            hc = xc * (1.0 + mc[4]) + mc[3]
            n_lat = B * L
            f_all = hier_moe(jnp.concatenate([h.reshape(-1, D), hc.reshape(-1, D)], axis=0), *moe_w)
            f = f_all[:n_lat].reshape(B, L, D)
            fc = f_all[n_lat:].reshape(xc.shape)
            xc = layer_norm(DEEPNORM_ALPHA * xc + mc[5] * fc, ln_g[layer, 1], ln_b[layer, 1])
        x = layer_norm(DEEPNORM_ALPHA * x + m[5] * f, ln_g[layer, 1], ln_b[layer, 1])
    return x
```

```python
import functools

import jax
import jax.numpy as jnp
from jax import lax
from jax.experimental import pallas as pl
from jax.experimental.pallas import tpu as pltpu

D_MODEL = 4096
DEPTH = 2
GRID_W = 64
NA_HEAD_DIM = 128
NA_HEADS = D_MODEL // 2 // NA_HEAD_DIM
NA_WIDTH = NA_HEADS * NA_HEAD_DIM
NA_WIN_ROWS = 8
NA_WIN_COLS = 16
RET_HEADS = 8
RET_V_DIM = D_MODEL // 2 // RET_HEADS
RET_QK_DIM = RET_V_DIM // 2
RET_QK_WIDTH = RET_HEADS * RET_QK_DIM
RET_V_WIDTH = RET_HEADS * RET_V_DIM
S5_GROUP_CH = 16
S5_STATE = 64
S5_T = 16
MOE_GROUPS = 4
MOE_EXPERTS_PER_GROUP = 8
MOE_EXPERTS = MOE_GROUPS * MOE_EXPERTS_PER_GROUP
MOE_TOP_K = 2
EXPERT_BLOCK = 256
ROPE_BASE = 10000.0
LN_EPS = 1e-5
DEEPNORM_ALPHA = (2.0 * DEPTH) ** 0.25
NEG_BIG = -1e30

VMEM_LIMIT_BYTES = 48 * 1024 * 1024
MM_TILE_M = 512
MM_TILE_N = 512
ROW_TILE = 256
RET_TILE = 256
NA_ROWS_PER_STEP = 4
HI = lax.Precision.HIGHEST
BF = jnp.bfloat16


def _cparams(n_axes):
    return pltpu.CompilerParams(dimension_semantics=("arbitrary",) * n_axes,
                                vmem_limit_bytes=VMEM_LIMIT_BYTES)


def _mm_body(x_ref, w_ref, o_ref):
    o_ref[...] = jnp.dot(x_ref[...], w_ref[...],
                         preferred_element_type=jnp.float32).astype(o_ref.dtype)


def _mm_glu_body(x_ref, w_ref, g_ref, o_ref):
    t = jnp.dot(x_ref[...], w_ref[...], preferred_element_type=jnp.float32)
    o_ref[...] = (g_ref[...].astype(jnp.float32) * jax.nn.sigmoid(t)).astype(o_ref.dtype)


def _mm(x, w, out_dtype=jnp.float32, glu_gate=None):
    m, k = x.shape
    n = w.shape[1]
    tm = MM_TILE_M if m % MM_TILE_M == 0 else m
    tn = MM_TILE_N if n % MM_TILE_N == 0 else n
    in_specs = [pl.BlockSpec((tm, k), lambda i, j: (i, 0)),
                pl.BlockSpec((k, tn), lambda i, j: (0, j))]
    args = [x, w]
    body = _mm_body
    if glu_gate is not None:
        in_specs.append(pl.BlockSpec((tm, tn), lambda i, j: (i, j)))
        args.append(glu_gate)
        body = _mm_glu_body
    return pl.pallas_call(
        body,
        grid=(m // tm, n // tn),
        in_specs=in_specs,
        out_specs=pl.BlockSpec((tm, tn), lambda i, j: (i, j)),
        out_shape=jax.ShapeDtypeStruct((m, n), out_dtype),
        compiler_params=_cparams(2),
        name="dense_mm",
    )(*args)


def _mod_row_spec(n_ctx_blocks, n_batch):
    return pl.BlockSpec((1, 1, D_MODEL), lambda b, t: (jnp.where(t < n_ctx_blocks, n_batch, b), 0, 0))


def _modulate_body(x_ref, shift_ref, scale_ref, h_ref):
    h_ref[0] = (x_ref[0] * (1.0 + scale_ref[0]) + shift_ref[0]).astype(h_ref.dtype)


def modulate(x, shift, scale, *, n_ctx):
    B, n, d = x.shape
    blk = pl.BlockSpec((1, ROW_TILE, d), lambda b, t: (b, t, 0))
    mod = _mod_row_spec(n_ctx // ROW_TILE, B)
    return pl.pallas_call(
        _modulate_body,
        grid=(B, n // ROW_TILE),
        in_specs=[blk, mod, mod],
        out_specs=blk,
        out_shape=jax.ShapeDtypeStruct((B, n, d), BF),
        compiler_params=_cparams(2),
        name="modulate",
    )(x, shift, scale)


def _ln_body(x_ref, o_ref, gate_ref, g_ref, b_ref, *rest, with_next):
    if with_next:
        shift_ref, scale_ref, x1_ref, h_ref = rest
    else:
        (x1_ref,) = rest
    y = DEEPNORM_ALPHA * x_ref[0] + gate_ref[0] * o_ref[0].astype(jnp.float32)
    mu = jnp.mean(y, axis=-1, keepdims=True)
    yc = y - mu
    var = jnp.mean(yc * yc, axis=-1, keepdims=True)
    x1 = yc * lax.rsqrt(var + LN_EPS) * g_ref[...] + b_ref[...]
    x1_ref[0] = x1
    if with_next:
        h_ref[0] = (x1 * (1.0 + scale_ref[0]) + shift_ref[0]).astype(h_ref.dtype)


def residual_ln(x, o, gate, ln_g, ln_b, *, n_ctx, next_shift=None, next_scale=None, latent_only=False):
    B, n, d = x.shape
    n_cb = n_ctx // ROW_TILE
    blk = pl.BlockSpec((1, ROW_TILE, d), lambda b, t: (b, t, 0))
    mod = _mod_row_spec(n_cb, B)
    vec = pl.BlockSpec((1, d), lambda b, t: (0, 0))
    with_next = next_shift is not None
    in_specs = [blk, blk, mod, vec, vec]
    args = [x, o, gate, ln_g.reshape(1, d), ln_b.reshape(1, d)]
    if latent_only:
        x1_spec = pl.BlockSpec((1, ROW_TILE, d), lambda b, t: (b, jnp.maximum(t - n_cb, 0), 0))
        x1_shape = jax.ShapeDtypeStruct((B, n - n_ctx, d), jnp.float32)
    else:
        x1_spec = blk
        x1_shape = jax.ShapeDtypeStruct((B, n, d), jnp.float32)
    out_specs, out_shape = x1_spec, x1_shape
    if with_next:
        in_specs += [mod, mod]
        args += [next_shift, next_scale]
        out_specs = [x1_spec, blk]
        out_shape = [x1_shape, jax.ShapeDtypeStruct((B, n, d), BF)]
    return pl.pallas_call(
        functools.partial(_ln_body, with_next=with_next),
        grid=(B, n // ROW_TILE),
        in_specs=in_specs,
        out_specs=out_specs,
        out_shape=out_shape,
        compiler_params=_cparams(2),
        name="residual_ln",
    )(*args)


def na_bias_table(rpb):
    H = rpb.shape[0]
    col = jnp.arange(GRID_W)
    col_start = jnp.clip(col - NA_WIN_COLS // 2, 0, GRID_W - NA_WIN_COLS)
    col_ok = (col[None, :] >= col_start[:, None]) & (col[None, :] < col_start[:, None] + NA_WIN_COLS)
    col_idx = jnp.clip(col[None, :] - col[:, None] + NA_WIN_COLS - 1, 0, 2 * NA_WIN_COLS - 2)
    rpb_col = rpb[:, :, col_idx]
    offs = jnp.arange(NA_WIN_ROWS)
    j = jnp.arange(NA_WIN_ROWS)
    row_idx = j[None, :] - offs[:, None] + (NA_WIN_ROWS - 1)
    b = rpb_col[:, row_idx]
    b = jnp.where(col_ok[None, None, None], b, NEG_BIG)
    b = b.transpose(0, 1, 3, 2, 4).reshape(H, NA_WIN_ROWS, GRID_W, NA_WIN_ROWS * GRID_W)
    return b.astype(jnp.float32)


def _na_body(q_ref, k_ref, v_ref, bias_ref, o_ref, *, n_ctx, rows, scale):
    t = pl.program_id(2)
    n_cb = n_ctx // (NA_ROWS_PER_STEP * GRID_W)
    kc = k_ref[0, 0:n_ctx, :]
    vc = v_ref[0, 0:n_ctx, :]
    dn_t = (((1,), (1,)), ((), ()))

    @pl.when(t < n_cb)
    def _ctx():
        s = lax.dot_general(q_ref[0], kc, dn_t, preferred_element_type=jnp.float32) * scale
        m = jnp.max(s, axis=-1, keepdims=True)
        p = jnp.exp(s - m)
        den = jnp.sum(p, axis=-1, keepdims=True)
        o = jnp.dot(p.astype(vc.dtype), vc, preferred_element_type=jnp.float32) / den
        o_ref[0] = o.astype(o_ref.dtype)

    @pl.when(t >= n_cb)
    def _lat():
        r0 = (t - n_cb) * NA_ROWS_PER_STEP
        n_loc = NA_WIN_ROWS * GRID_W
        for i in range(NA_ROWS_PER_STEP):
            r = r0 + i
            rs = jnp.clip(r - NA_WIN_ROWS // 2, 0, rows - NA_WIN_ROWS)
            qi = q_ref[0, i * GRID_W:(i + 1) * GRID_W, :]
            start = pl.multiple_of(n_ctx + rs * GRID_W, GRID_W)
            kl = k_ref[0, pl.ds(start, n_loc), :]
            vl = v_ref[0, pl.ds(start, n_loc), :]
            s_loc = lax.dot_general(qi, kl, dn_t, preferred_element_type=jnp.float32) * scale
            s_loc = s_loc + bias_ref[0, r - rs]
            s_ctx = lax.dot_general(qi, kc, dn_t, preferred_element_type=jnp.float32) * scale
            m = jnp.maximum(jnp.max(s_loc, axis=-1, keepdims=True), jnp.max(s_ctx, axis=-1, keepdims=True))
            p_loc = jnp.exp(s_loc - m)
            p_ctx = jnp.exp(s_ctx - m)
            den = jnp.sum(p_loc, axis=-1, keepdims=True) + jnp.sum(p_ctx, axis=-1, keepdims=True)
            o = (jnp.dot(p_loc.astype(vl.dtype), vl, preferred_element_type=jnp.float32)
                 + jnp.dot(p_ctx.astype(vc.dtype), vc, preferred_element_type=jnp.float32)) / den
            o_ref[0, i * GRID_W:(i + 1) * GRID_W, :] = o.astype(o_ref.dtype)


def na_attention(z, bias, *, n_heads, head_dim, n_ctx, q_col, k_col, v_col, out_dtype=BF):
    B, ltot, _ = z.shape
    L = ltot - n_ctx
    rows = L // GRID_W
    tq = NA_ROWS_PER_STEP * GRID_W
    assert n_ctx % tq == 0 and L % tq == 0 and rows >= NA_WIN_ROWS
    body = functools.partial(_na_body, n_ctx=n_ctx, rows=rows, scale=head_dim ** -0.5)
    return pl.pallas_call(
        body,
        grid=(B, n_heads, ltot // tq),
        in_specs=[
            pl.BlockSpec((1, tq, head_dim), lambda b, h, t: (b, t, q_col + h)),
            pl.BlockSpec((1, ltot, head_dim), lambda b, h, t: (b, 0, k_col + h)),
            pl.BlockSpec((1, ltot, head_dim), lambda b, h, t: (b, 0, v_col + h)),
            pl.BlockSpec((1, NA_WIN_ROWS, GRID_W, NA_WIN_ROWS * GRID_W), lambda b, h, t: (h, 0, 0, 0)),
        ],
        out_specs=pl.BlockSpec((1, tq, head_dim), lambda b, h, t: (b, t, h)),
        out_shape=jax.ShapeDtypeStruct((B, ltot, n_heads * head_dim), out_dtype),
        compiler_params=_cparams(3),
        name="na_attention",
    )(z, z, z, bias)


def rope_tables(n_ctx, L, head_dim):
    n_freq = head_dim // 4
    freq = ROPE_BASE ** (-jnp.arange(n_freq, dtype=jnp.float32) / n_freq)
    t = jnp.arange(L)
    ang_r = (t // GRID_W).astype(jnp.float32)[:, None] * freq[None, :]
    ang_c = (t % GRID_W).astype(jnp.float32)[:, None] * freq[None, :]
    cos = jnp.concatenate([jnp.cos(ang_r), jnp.cos(ang_r), jnp.cos(ang_c), jnp.cos(ang_c)], axis=-1)
    sin = jnp.concatenate([-jnp.sin(ang_r), jnp.sin(ang_r), -jnp.sin(ang_c), jnp.sin(ang_c)], axis=-1)
    cos = jnp.concatenate([jnp.ones((n_ctx, head_dim), jnp.float32), cos], axis=0)
    sin = jnp.concatenate([jnp.zeros((n_ctx, head_dim), jnp.float32), sin], axis=0)
    return cos, sin


def _rope(x, cos, sin):
    d = x.shape[-1]
    quarter = d // 4
    lane = lax.broadcasted_iota(jnp.int32, x.shape, 1)
    first = (lane % (2 * quarter)) < quarter
    partner = jnp.where(first, pltpu.roll(x, d - quarter, 1), pltpu.roll(x, quarter, 1))
    return x * cos + partner * sin


def _ret_body(lg_ref, q_ref, k_ref, v_ref, cos_ref, sin_ref, *rest, reverse, k_scale):
    if reverse:
        of_ref, gate_ref, y_ref, s_ref = rest
    else:
        o_ref, s_ref = rest
    h = pl.program_id(1)
    c = pl.program_id(2)
    C = q_ref.shape[1]

    @pl.when(c == 0)
    def _init():
        s_ref[...] = jnp.zeros_like(s_ref)

    lg = lg_ref[h]
    cos = cos_ref[...]
    sin = sin_ref[...]
    q = _rope(q_ref[0].astype(jnp.float32), cos, sin)
    k = _rope(k_ref[0].astype(jnp.float32), cos, sin) * k_scale
    v = v_ref[0]
    pos = lax.broadcasted_iota(jnp.int32, (C, 1), 0).astype(jnp.float32)
    row = lax.broadcasted_iota(jnp.int32, (C, C), 0).astype(jnp.float32)
    colm = lax.broadcasted_iota(jnp.int32, (C, C), 1).astype(jnp.float32)
    if reverse:
        pos = (C - 1.0) - pos
        rel = colm - row
    else:
        rel = row - colm
    intra = jnp.where(rel >= 0, jnp.exp(lg * jnp.maximum(rel, 0.0)), 0.0)
    q_dec = jnp.exp(lg * (pos + 1.0))
    k_dec = jnp.exp(lg * ((C - 1.0) - pos))
    c_dec = jnp.exp(lg * C)
    scores = lax.dot_general(q.astype(BF), k.astype(BF), (((1,), (1,)), ((), ())),
                             preferred_element_type=jnp.float32) * intra
    s_old = s_ref[...]
    o = (jnp.dot(scores.astype(BF), v, preferred_element_type=jnp.float32)
         + jnp.dot((q * q_dec).astype(BF), s_old.astype(BF), preferred_element_type=jnp.float32))
    s_ref[...] = c_dec * s_old + lax.dot_general((k * k_dec).astype(BF), v, (((0,), (0,)), ((), ())),
                                                 preferred_element_type=jnp.float32)
    if reverse:
        o = o + of_ref[0]
        mu = jnp.mean(o, axis=-1, keepdims=True)
        oc = o - mu
        var = jnp.mean(oc * oc, axis=-1, keepdims=True)
        g = gate_ref[0].astype(jnp.float32)
        y_ref[0] = (oc * lax.rsqrt(var + LN_EPS) * (g * jax.nn.sigmoid(g))).astype(y_ref.dtype)
    else:
        o_ref[0] = o


def retention(z, cos, sin, log_dec_f, log_dec_b, *, n_heads, qk_dim, v_dim, n_ctx,
              q_col, k_col, v_col, g_col, out_dtype=BF):
    B, ltot, _ = z.shape
    C = RET_TILE
    assert n_ctx % C == 0 and ltot % C == 0
    n_c = ltot // C
    n_cc = n_ctx // C
    k_scale = qk_dim ** -0.5

    def fwd_idx(c):
        return c

    def bwd_idx(c):
        return jnp.where(c < n_cc, n_cc - 1 - c, n_c - 1 - (c - n_cc))

    def specs(cidx):
        return [
            pl.BlockSpec((1, C, qk_dim), lambda b, h, c, lg: (b, cidx(c), q_col + h)),
            pl.BlockSpec((1, C, qk_dim), lambda b, h, c, lg: (b, cidx(c), k_col + h)),
            pl.BlockSpec((1, C, v_dim), lambda b, h, c, lg: (b, cidx(c), v_col + h)),
            pl.BlockSpec((C, qk_dim), lambda b, h, c, lg: (cidx(c), 0)),
            pl.BlockSpec((C, qk_dim), lambda b, h, c, lg: (cidx(c), 0)),
        ]

    o_f = pl.pallas_call(
        functools.partial(_ret_body, reverse=False, k_scale=k_scale),
        grid_spec=pltpu.PrefetchScalarGridSpec(
            num_scalar_prefetch=1, grid=(B, n_heads, n_c),
            in_specs=specs(fwd_idx),
            out_specs=pl.BlockSpec((1, C, v_dim), lambda b, h, c, lg: (b, c, h)),
            scratch_shapes=[pltpu.VMEM((qk_dim, v_dim), jnp.float32)]),
        out_shape=jax.ShapeDtypeStruct((B, ltot, n_heads * v_dim), jnp.float32),
        compiler_params=_cparams(3), name="retention_fwd",
    )(log_dec_f.astype(jnp.float32), z, z, z, cos, sin)

    return pl.pallas_call(
        functools.partial(_ret_body, reverse=True, k_scale=k_scale),
        grid_spec=pltpu.PrefetchScalarGridSpec(
            num_scalar_prefetch=1, grid=(B, n_heads, n_c),
            in_specs=specs(bwd_idx) + [
                pl.BlockSpec((1, C, v_dim), lambda b, h, c, lg: (b, bwd_idx(c), h)),
                pl.BlockSpec((1, C, v_dim), lambda b, h, c, lg: (b, bwd_idx(c), g_col + h)),
            ],
            out_specs=pl.BlockSpec((1, C, v_dim), lambda b, h, c, lg: (b, bwd_idx(c), h)),
            scratch_shapes=[pltpu.VMEM((qk_dim, v_dim), jnp.float32)]),
        out_shape=jax.ShapeDtypeStruct((B, ltot, n_heads * v_dim), out_dtype),
        compiler_params=_cparams(3), name="retention_bwd",
    )(log_dec_b.astype(jnp.float32), z, z, z, cos, sin, o_f, z)


RET_CHUNK = 128


def _head_norm(x):
    mu = x.mean(-1, keepdims=True)
    var = jnp.square(x - mu).mean(-1, keepdims=True)
    return (x - mu) * lax.rsqrt(var + LN_EPS)


def _axial_rope(x, pos_row, pos_col):
    half = x.shape[-1] // 2
    n_freq = half // 2
    freq = ROPE_BASE ** (-jnp.arange(n_freq, dtype=jnp.float32) / n_freq)

    def rotate(xp, pos):
        ang = pos.astype(jnp.float32)[:, None] * freq[None, :]
        cos = jnp.cos(ang)[None, :, None, :]
        sin = jnp.sin(ang)[None, :, None, :]
        x1, x2 = xp[..., :n_freq], xp[..., n_freq:]
        return jnp.concatenate([x1 * cos - x2 * sin, x1 * sin + x2 * cos], axis=-1)

    return jnp.concatenate([rotate(x[..., :half], pos_row), rotate(x[..., half:], pos_col)], axis=-1)


def _retention_chunkwise(q, k, v, log_gamma, s0, with_output):
    B, L, H, _ = q.shape
    dv = v.shape[-1]
    n = L // RET_CHUNK
    idx = jnp.arange(RET_CHUNK, dtype=jnp.float32)
    rel = idx[:, None] - idx[None, :]
    intra = jnp.where(rel >= 0, jnp.exp(log_gamma[:, None, None] * jnp.maximum(rel, 0.0)), 0.0)
    q_dec = jnp.exp(log_gamma[None, :] * (idx[:, None] + 1.0))[None, :, :, None]
    k_dec = jnp.exp(log_gamma[None, :] * (RET_CHUNK - 1.0 - idx[:, None]))[None, :, :, None]
    c_dec = jnp.exp(log_gamma * RET_CHUNK)[None, :, None, None]

    def blocks(z):
        return z.reshape(B, n, RET_CHUNK, H, z.shape[-1]).swapaxes(0, 1)

    def step(s, qkv):
        qi, ki, vi = qkv
        s_new = c_dec * s + jnp.einsum('bchd,bche->bhde', ki * k_dec, vi)
        if not with_output:
            return s_new, None
        scores = jnp.einsum('bqhd,bkhd->bhqk', qi, ki) * intra[None]
        o = (jnp.einsum('bhqk,bkhe->bqhe', scores, vi)
             + jnp.einsum('bqhd,bhde->bqhe', qi * q_dec, s))
        return s_new, o

    s_fin, o = lax.scan(step, s0, (blocks(q), blocks(k), blocks(v)))
    if with_output:
        o = o.swapaxes(0, 1).reshape(B, L, H, dv)
    return o, s_fin


def _bidirectional_retention(q, k, v, qc, kc, vc, log_dec_f, log_dec_b):
    s0 = jnp.zeros((q.shape[0], RET_HEADS, RET_QK_DIM, RET_V_DIM), jnp.float32)
    lf = log_dec_f.astype(jnp.float32)
    lb = log_dec_b.astype(jnp.float32)

    def flip(t):
        return t[:, ::-1]

    oc_f, s_cf = _retention_chunkwise(qc, kc, vc, lf, s0, True)
    o_f, _ = _retention_chunkwise(q, k, v, lf, s_cf, True)
    oc_b, s_cb = _retention_chunkwise(flip(qc), flip(kc), flip(vc), lb, s0, True)
    o_b, _ = _retention_chunkwise(flip(q), flip(k), flip(v), lb, s_cb, True)
    return o_f + flip(o_b), oc_f + flip(oc_b)


def retention_jax(z, log_dec_f, log_dec_b, n_ctx):
    B, ltot, _ = z.shape
    L = ltot - n_ctx
    zf = z.astype(jnp.float32)
    base = 3 * NA_WIDTH
    sp = [base, base + RET_QK_WIDTH, base + 2 * RET_QK_WIDTH, base + 2 * RET_QK_WIDTH + RET_V_WIDTH,
          base + 2 * RET_QK_WIDTH + 2 * RET_V_WIDTH]
    zl = [zf[:, n_ctx:, sp[i]:sp[i + 1]] for i in range(4)]
    zc = [zf[:, :n_ctx, sp[i]:sp[i + 1]] for i in range(4)]

    def heads(t):
        return t.reshape(B, t.shape[1], RET_HEADS, -1)

    t = jnp.arange(L)
    pos_row = t // GRID_W
    pos_col = t % GRID_W
    k_scale = RET_QK_DIM ** -0.5
    qb = _axial_rope(heads(zl[0]), pos_row, pos_col)
    kb = _axial_rope(heads(zl[1]), pos_row, pos_col) * k_scale
    o, oc = _bidirectional_retention(qb, kb, heads(zl[2]), heads(zc[0]), heads(zc[1]) * k_scale, heads(zc[2]),
                                     log_dec_f, log_dec_b)

    def merge(o_b, gate):
        return _head_norm(o_b).reshape(B, o_b.shape[1], RET_V_WIDTH) * jax.nn.silu(gate)

    return jnp.concatenate([merge(oc, zc[3]), merge(o, zl[3])], axis=1).astype(BF)


def _s5_discretize(lam_re, lam_im, log_dt, b_re, b_im):
    dt = jnp.exp(log_dt)[:, None]
    logmag = lam_re * dt
    angle = lam_im * dt
    nr = jnp.exp(logmag) * jnp.cos(angle) - 1.0
    ni = jnp.exp(logmag) * jnp.sin(angle)
    den = lam_re * lam_re + lam_im * lam_im
    fr = ((nr * lam_re + ni * lam_im) / den)[..., None]
    fi = ((ni * lam_re - nr * lam_im) / den)[..., None]
    return logmag, angle, fr * b_re - fi * b_im, fr * b_im + fi * b_re


def s5_operators(lam_re, lam_im, log_dt, b_re, b_im, c_re, c_im):
    T, CH, P = S5_T, S5_GROUP_CH, S5_STATE
    G = lam_re.shape[1]
    tau = jnp.arange(T + 1, dtype=jnp.float32)[:, None, None]
    t_i = jnp.arange(T)
    kt = 0.0
    wbs, wcs, a_rows = [], [], []
    for d in range(2):
        logmag, angle, bb_re, bb_im = _s5_discretize(lam_re[d], lam_im[d], log_dt[d], b_re, b_im)
        mag = jnp.exp(logmag[None] * tau)
        pr = mag * jnp.cos(angle[None] * tau)
        pi = mag * jnp.sin(angle[None] * tau)
        cr, ci = c_re[d], c_im[d]
        z_re = pr[..., None] * bb_re[None] - pi[..., None] * bb_im[None]
        z_im = pr[..., None] * bb_im[None] + pi[..., None] * bb_re[None]
        m = (jnp.einsum('ghp,tgpk->tghk', cr, z_re, precision=HI)
             - jnp.einsum('ghp,tgpk->tghk', ci, z_im, precision=HI))
        lag = (t_i[:, None] - t_i[None, :]) if d == 0 else (t_i[None, :] - t_i[:, None])
        blk = jnp.where((lag >= 0)[:, :, None, None, None], m[jnp.clip(lag, 0, T)], 0.0)
        kt = kt + blk.transpose(2, 1, 4, 0, 3).reshape(G, T * CH, T * CH)
        dist = (T - 1 - t_i) if d == 0 else t_i
        wbs += [z_re[dist].transpose(1, 0, 3, 2).reshape(G, T * CH, P),
                z_im[dist].transpose(1, 0, 3, 2).reshape(G, T * CH, P)]
        n_t = (t_i + 1) if d == 0 else (T - t_i)
        qr = pr[n_t][:, :, None, :]
        qi = pi[n_t][:, :, None, :]
        wcs += [(cr[None] * qr - ci[None] * qi).transpose(1, 3, 0, 2).reshape(G, P, T * CH),
                (-(cr[None] * qi + ci[None] * qr)).transpose(1, 3, 0, 2).reshape(G, P, T * CH)]
        a_rows += [pr[T].reshape(-1), pi[T].reshape(-1)]

    def pair_diag(w):
        g2 = w.reshape(G // 2, 2, w.shape[1], w.shape[2])
        z = jnp.zeros_like(g2[:, 0])
        return jnp.concatenate([jnp.concatenate([g2[:, 0], z], axis=2),
                                jnp.concatenate([z, g2[:, 1]], axis=2)], axis=1)

    wb = jnp.concatenate([pair_diag(w) for w in wbs], axis=2)
    wc = jnp.concatenate([pair_diag(w) for w in wcs], axis=1)
    return kt.astype(BF), wb.astype(BF), wc.astype(BF), jnp.stack(a_rows)


def _s5_body(u_ref, kt_ref, wb_ref, wc_ref, a_ref, d_ref, y_ref, e_ref, x_ref, *, n_batch, n_c, n_cc, gb):
    P2 = 2 * S5_STATE
    W = S5_T * S5_GROUP_CH
    n_pair = gb // 2
    for q in range(n_pair):
        up = u_ref[:, q * 2 * W:(q + 1) * 2 * W].astype(BF)
        e = jnp.dot(up, wb_ref[q], preferred_element_type=jnp.float32)
        for comp in range(4):
            e_ref[comp, :, q * P2:(q + 1) * P2] = e[:, comp * P2:(comp + 1) * P2]
    a_fr, a_fi, a_br, a_bi = a_ref[0:1, :], a_ref[1:2, :], a_ref[2:3, :], a_ref[3:4, :]

    def step(i, carry):
        j = jnp.where(i < n_cc, n_cc - 1 - i, n_c - 1 - (i - n_cc))
        new = []
        for b in range(n_batch):
            xfr, xfi, xbr, xbi = carry[b]
            rf = b * n_c + i
            rb = b * n_c + j
            x_ref[0, pl.ds(rf, 1), :] = xfr
            x_ref[1, pl.ds(rf, 1), :] = xfi
            x_ref[2, pl.ds(rb, 1), :] = xbr
            x_ref[3, pl.ds(rb, 1), :] = xbi
            new.append((a_fr * xfr - a_fi * xfi + e_ref[0, pl.ds(rf, 1), :],
                        a_fr * xfi + a_fi * xfr + e_ref[1, pl.ds(rf, 1), :],
                        a_br * xbr - a_bi * xbi + e_ref[2, pl.ds(rb, 1), :],
                        a_br * xbi + a_bi * xbr + e_ref[3, pl.ds(rb, 1), :]))
        return tuple(new)

    zero = jnp.zeros((1, n_pair * P2), jnp.float32)
    lax.fori_loop(0, n_c, step, tuple((zero, zero, zero, zero) for _ in range(n_batch)))
    for q in range(n_pair):
        xcat = jnp.concatenate([x_ref[comp, :, q * P2:(q + 1) * P2] for comp in range(4)], axis=1).astype(BF)
        yx = jnp.dot(xcat, wc_ref[q], preferred_element_type=jnp.float32)
        for gi in range(2):
            g = 2 * q + gi
            u = u_ref[:, g * W:(g + 1) * W]
            y = (jnp.dot(u.astype(BF), kt_ref[g], preferred_element_type=jnp.float32)
                 + yx[:, gi * W:(gi + 1) * W] + d_ref[:, g * W:(g + 1) * W] * u)
            y_ref[:, g * W:(g + 1) * W] = jax.nn.gelu(y)


def s5_scan_gelu(u, ops, d_skip, *, n_ctx, gb=4):
    kt, wb, wc, a = ops
    B, ltot, D = u.shape
    T, CH, P = S5_T, S5_GROUP_CH, S5_STATE
    G = D // CH
    W = T * CH
    n_c = ltot // T
    R = B * n_c
    u2 = u.reshape(B, n_c, T, G, CH).transpose(0, 1, 3, 2, 4).reshape(R, G * W)
    d2 = jnp.broadcast_to(d_skip.reshape(G, 1, CH), (G, T, CH)).reshape(1, G * W)
    body = functools.partial(_s5_body, n_batch=B, n_c=n_c, n_cc=n_ctx // T, gb=gb)
    y2 = pl.pallas_call(
        body,
        grid=(G // gb,),
        in_specs=[
            pl.BlockSpec((R, gb * W), lambda i: (0, i)),
            pl.BlockSpec((gb, W, W), lambda i: (i, 0, 0)),
            pl.BlockSpec((gb // 2, 2 * W, 8 * P), lambda i: (i, 0, 0)),
            pl.BlockSpec((gb // 2, 8 * P, 2 * W), lambda i: (i, 0, 0)),
            pl.BlockSpec((4, gb * P), lambda i: (0, i)),
            pl.BlockSpec((1, gb * W), lambda i: (0, i)),
        ],
        out_specs=pl.BlockSpec((R, gb * W), lambda i: (0, i)),
        out_shape=jax.ShapeDtypeStruct((R, G * W), jnp.float32),
        scratch_shapes=[pltpu.VMEM((4, R, gb * P), jnp.float32),
                        pltpu.VMEM((4, R, gb * P), jnp.float32)],
        compiler_params=_cparams(1),
        name="s5_scan",
    )(u2, kt, wb, wc, a, d2)
    return y2.reshape(B, n_c, G, T, CH).transpose(0, 1, 3, 2, 4).reshape(B, ltot, D)


def routed_experts(h, eid, wts, w_gate, w_up, w_down):
    n_tok, d = h.shape
    n_asg = n_tok * MOE_TOP_K
    flat_e = eid.reshape(-1)
    order = jnp.argsort(flat_e)
    e_sorted = flat_e[order]
    counts = jnp.bincount(flat_e, length=MOE_EXPERTS)
    padded = (counts + EXPERT_BLOCK - 1) // EXPERT_BLOCK * EXPERT_BLOCK
    pad_end = jnp.cumsum(padded)
    pad_start = pad_end - padded
    start = jnp.cumsum(counts) - counts
    dest = pad_start[e_sorted] + jnp.arange(n_asg) - start[e_sorted]
    n_blocks = (n_asg + MOE_EXPERTS * (EXPERT_BLOCK - 1)) // EXPERT_BLOCK + 1
    n_rows = n_blocks * EXPERT_BLOCK
    row_tok = jnp.full((n_rows,), n_tok, jnp.int32).at[dest].set((order // MOE_TOP_K).astype(jnp.int32))
    row_w = jnp.zeros((n_rows,), jnp.float32).at[dest].set(wts.reshape(-1)[order])
    blk_e = jnp.minimum(jnp.searchsorted(pad_end, jnp.arange(n_blocks) * EXPERT_BLOCK, side='right'),
                        MOE_EXPERTS - 1)
    h_pad = jnp.concatenate([h, jnp.zeros((1, d), h.dtype)], axis=0)
    xb = h_pad[row_tok].reshape(n_blocks, EXPERT_BLOCK, d)

    def expert_block(args):
        xblk, e = args
        return (jax.nn.silu(xblk @ w_gate[e]) * (xblk @ w_up[e])) @ w_down[e]

    yb = lax.map(expert_block, (xb, blk_e)).reshape(n_rows, d)
    out = jax.ops.segment_sum(yb.astype(jnp.float32) * row_w[:, None], row_tok, num_segments=n_tok + 1)
    return out[:n_tok].astype(h.dtype)


def hier_moe(h, w_group, b_group, w_expert, b_expert, w_gate, w_up, w_down):
    n_tok = h.shape[0]
    g_prob = jax.nn.softmax((h @ w_group) + b_group, axis=-1)
    g_w, g_idx = lax.top_k(g_prob, 1)
    e_logits = ((h @ w_expert) + b_expert).reshape(n_tok, MOE_GROUPS, MOE_EXPERTS_PER_GROUP)
    e_in = jnp.take_along_axis(e_logits, g_idx[:, :, None], axis=1)[:, 0]
    top_l, top_i = lax.top_k(e_in, MOE_TOP_K)
    wts = g_w * jax.nn.softmax(top_l, axis=-1)
    eid = g_idx * MOE_EXPERTS_PER_GROUP + top_i
    return routed_experts(h, eid, wts, w_gate, w_up, w_down)


def kernel(x, c, ctx, c_ctx, mod_w, mod_b, ln_g, ln_b, mix_w_in, mix_w_out, na_rpb,
           ret_log_decay_fwd, ret_log_decay_bwd, s5_w_in, s5_lam_re, s5_lam_im, s5_log_dt,
           s5_b_re, s5_b_im, s5_c_re, s5_c_im, s5_d, s5_w_glu, s5_w_out,
           moe_w_group, moe_b_group, moe_w_expert, moe_b_expert, moe_w_gate, moe_w_up, moe_w_down):
    B, L, D = x.shape
    n_ctx = ctx.shape[1]
    ltot = n_ctx + L
    n_tok = B * ltot
    xa = jnp.concatenate([ctx, x], axis=1)
    cond = jnp.concatenate([jax.nn.silu(c), jax.nn.silu(c_ctx)[None]], axis=0)
    mods = []
    for layer in range(DEPTH):
        m = (cond @ mod_w[layer] + mod_b[layer]).reshape(B + 1, 6, 1, D)
        mods.append([m[:, i] for i in range(6)])
    cos, sin = rope_tables(n_ctx, L, RET_QK_DIM)

    h = modulate(xa, mods[0][0], mods[0][1], n_ctx=n_ctx)
    for layer in range(DEPTH):
        last = layer == DEPTH - 1
        m = mods[layer]
        j = layer // 2
        h2d = h.reshape(n_tok, D)
        if layer % 2 == 0:
            z = _mm(h2d, mix_w_in[j].astype(BF), BF).reshape(B, ltot, -1)
            o_na = na_attention(z, na_bias_table(na_rpb[j]), n_heads=NA_HEADS, head_dim=NA_HEAD_DIM,
                                n_ctx=n_ctx, q_col=0, k_col=NA_HEADS, v_col=2 * NA_HEADS)
            y_ret = retention_jax(z, ret_log_decay_fwd[j], ret_log_decay_bwd[j], n_ctx)
            y = jnp.concatenate([o_na, y_ret], axis=-1).reshape(n_tok, D)
            o = _mm(y, mix_w_out[j].astype(BF))
        else:
            u = _mm(h2d, s5_w_in[j].astype(BF)).reshape(B, ltot, D)
            ops = s5_operators(s5_lam_re[j], s5_lam_im[j], s5_log_dt[j], s5_b_re[j], s5_b_im[j],
                               s5_c_re[j], s5_c_im[j])
            g = s5_scan_gelu(u, ops, s5_d[j], n_ctx=n_ctx).astype(BF).reshape(n_tok, D)
            gg = _mm(g, s5_w_glu[j].astype(BF), BF, glu_gate=g)
            o = _mm(gg, s5_w_out[j].astype(BF))
        x1, h_moe = residual_ln(xa, o.reshape(B, ltot, D), m[2], ln_g[layer, 0], ln_b[layer, 0],
                                n_ctx=n_ctx, next_shift=m[3], next_scale=m[4])
        f = hier_moe(h_moe.reshape(n_tok, D).astype(jnp.float32),
                     moe_w_group[layer], moe_b_group[layer], moe_w_expert[layer], moe_b_expert[layer],
                     moe_w_gate[layer], moe_w_up[layer], moe_w_down[layer]).reshape(B, ltot, D)
        if last:
            return residual_ln(x1, f, m[5], ln_g[layer, 1], ln_b[layer, 1], n_ctx=n_ctx, latent_only=True)
        xa, h = residual_ln(x1, f, m[5], ln_g[layer, 1], ln_b[layer, 1], n_ctx=n_ctx,
                            next_shift=mods[layer + 1][0], next_scale=mods[layer + 1][1])
```

```python
import functools

import jax
import jax.numpy as jnp
from jax import lax
from jax.experimental import pallas as pl
from jax.experimental.pallas import tpu as pltpu

D_MODEL = 4096
DEPTH = 2
GRID_W = 64
NA_HEAD_DIM = 128
NA_HEADS = D_MODEL // 2 // NA_HEAD_DIM
NA_WIDTH = NA_HEADS * NA_HEAD_DIM
NA_WIN_ROWS = 8
NA_WIN_COLS = 16
RET_HEADS = 8
RET_V_DIM = D_MODEL // 2 // RET_HEADS
RET_QK_DIM = RET_V_DIM // 2
RET_QK_WIDTH = RET_HEADS * RET_QK_DIM
RET_V_WIDTH = RET_HEADS * RET_V_DIM
S5_GROUP_CH = 16
S5_STATE = 64
S5_T = 16
LANES = 128
S5_TILE_GROUPS = LANES // S5_GROUP_CH
MOE_GROUPS = 4
MOE_EXPERTS_PER_GROUP = 8
MOE_EXPERTS = MOE_GROUPS * MOE_EXPERTS_PER_GROUP
MOE_TOP_K = 2
EXPERT_BLOCK = 256
ROPE_BASE = 10000.0
LN_EPS = 1e-5
DEEPNORM_ALPHA = (2.0 * DEPTH) ** 0.25
NEG_BIG = -1e30

VMEM_LIMIT_BYTES = 48 * 1024 * 1024
MM_TILE_M = 512
MM_TILE_N = 512
ROW_TILE = 256
RET_TILE = 256
NA_ROWS_PER_STEP = 4
HI = lax.Precision.HIGHEST
BF = jnp.bfloat16


def _cparams(n_axes):
    return pltpu.CompilerParams(dimension_semantics=("arbitrary",) * n_axes,
                                vmem_limit_bytes=VMEM_LIMIT_BYTES)


def _mm_body(x_ref, w_ref, o_ref):
    o_ref[...] = jnp.dot(x_ref[...], w_ref[...],
                         preferred_element_type=jnp.float32).astype(o_ref.dtype)


def _mm_glu_body(x_ref, w_ref, g_ref, o_ref):
    t = jnp.dot(x_ref[...], w_ref[...], preferred_element_type=jnp.float32)
    o_ref[...] = (g_ref[...].astype(jnp.float32) * jax.nn.sigmoid(t)).astype(o_ref.dtype)


def _mm(x, w, out_dtype=jnp.float32, glu_gate=None):
    m, k = x.shape
    n = w.shape[1]
    tm = MM_TILE_M if m % MM_TILE_M == 0 else m
    tn = MM_TILE_N if n % MM_TILE_N == 0 else n
    in_specs = [pl.BlockSpec((tm, k), lambda i, j: (i, 0)),
                pl.BlockSpec((k, tn), lambda i, j: (0, j))]
    args = [x, w]
    body = _mm_body
    if glu_gate is not None:
        in_specs.append(pl.BlockSpec((tm, tn), lambda i, j: (i, j)))
        args.append(glu_gate)
        body = _mm_glu_body
    return pl.pallas_call(
        body,
        grid=(m // tm, n // tn),
        in_specs=in_specs,
        out_specs=pl.BlockSpec((tm, tn), lambda i, j: (i, j)),
        out_shape=jax.ShapeDtypeStruct((m, n), out_dtype),
        compiler_params=_cparams(2),
        name="dense_mm",
    )(*args)


def _mod_row_spec(n_ctx_blocks, n_batch):
    return pl.BlockSpec((1, 1, D_MODEL), lambda b, t: (jnp.where(t < n_ctx_blocks, n_batch, b), 0, 0))


def _modulate_body(x_ref, shift_ref, scale_ref, h_ref):
    h_ref[0] = (x_ref[0] * (1.0 + scale_ref[0]) + shift_ref[0]).astype(h_ref.dtype)


def modulate(x, shift, scale, *, n_ctx):
    B, n, d = x.shape
    blk = pl.BlockSpec((1, ROW_TILE, d), lambda b, t: (b, t, 0))
    mod = _mod_row_spec(n_ctx // ROW_TILE, B)
    return pl.pallas_call(
        _modulate_body,
        grid=(B, n // ROW_TILE),
        in_specs=[blk, mod, mod],
        out_specs=blk,
        out_shape=jax.ShapeDtypeStruct((B, n, d), BF),
        compiler_params=_cparams(2),
        name="modulate",
    )(x, shift, scale)


def _ln_body(x_ref, o_ref, gate_ref, g_ref, b_ref, *rest, with_next):
    if with_next:
        shift_ref, scale_ref, x1_ref, h_ref = rest
    else:
        (x1_ref,) = rest
    y = DEEPNORM_ALPHA * x_ref[0] + gate_ref[0] * o_ref[0].astype(jnp.float32)
    mu = jnp.mean(y, axis=-1, keepdims=True)
    yc = y - mu
    var = jnp.mean(yc * yc, axis=-1, keepdims=True)
    x1 = yc * lax.rsqrt(var + LN_EPS) * g_ref[...] + b_ref[...]
    x1_ref[0] = x1
    if with_next:
        h_ref[0] = (x1 * (1.0 + scale_ref[0]) + shift_ref[0]).astype(h_ref.dtype)


def residual_ln(x, o, gate, ln_g, ln_b, *, n_ctx, next_shift=None, next_scale=None, latent_only=False):
    B, n, d = x.shape
    n_cb = n_ctx // ROW_TILE
    blk = pl.BlockSpec((1, ROW_TILE, d), lambda b, t: (b, t, 0))
    mod = _mod_row_spec(n_cb, B)
    vec = pl.BlockSpec((1, d), lambda b, t: (0, 0))
    with_next = next_shift is not None
    in_specs = [blk, blk, mod, vec, vec]
    args = [x, o, gate, ln_g.reshape(1, d), ln_b.reshape(1, d)]
    if latent_only:
        x1_spec = pl.BlockSpec((1, ROW_TILE, d), lambda b, t: (b, jnp.maximum(t - n_cb, 0), 0))
        x1_shape = jax.ShapeDtypeStruct((B, n - n_ctx, d), jnp.float32)
    else:
        x1_spec = blk
        x1_shape = jax.ShapeDtypeStruct((B, n, d), jnp.float32)
    out_specs, out_shape = x1_spec, x1_shape
    if with_next:
        in_specs += [mod, mod]
        args += [next_shift, next_scale]
        out_specs = [x1_spec, blk]
        out_shape = [x1_shape, jax.ShapeDtypeStruct((B, n, d), BF)]
    return pl.pallas_call(
        functools.partial(_ln_body, with_next=with_next),
        grid=(B, n // ROW_TILE),
        in_specs=in_specs,
        out_specs=out_specs,
        out_shape=out_shape,
        compiler_params=_cparams(2),
        name="residual_ln",
    )(*args)


def na_bias_table(rpb):
    H = rpb.shape[0]
    col = jnp.arange(GRID_W)
    col_start = jnp.clip(col - NA_WIN_COLS // 2, 0, GRID_W - NA_WIN_COLS)
    col_ok = (col[None, :] >= col_start[:, None]) & (col[None, :] < col_start[:, None] + NA_WIN_COLS)
    col_idx = jnp.clip(col[None, :] - col[:, None] + NA_WIN_COLS - 1, 0, 2 * NA_WIN_COLS - 2)
    rpb_col = rpb[:, :, col_idx]
    offs = jnp.arange(NA_WIN_ROWS)
    j = jnp.arange(NA_WIN_ROWS)
    row_idx = j[None, :] - offs[:, None] + (NA_WIN_ROWS - 1)
    b = rpb_col[:, row_idx]
    b = jnp.where(col_ok[None, None, None], b, NEG_BIG)
    b = b.transpose(0, 1, 3, 2, 4).reshape(H, NA_WIN_ROWS, GRID_W, NA_WIN_ROWS * GRID_W)
    return b.astype(jnp.float32)


def _na_body(q_ref, k_ref, v_ref, bias_ref, o_ref, *, n_ctx, rows, scale):
    t = pl.program_id(2)
    n_cb = n_ctx // (NA_ROWS_PER_STEP * GRID_W)
    kc = k_ref[0, 0:n_ctx, :]
    vc = v_ref[0, 0:n_ctx, :]
    dn_t = (((1,), (1,)), ((), ()))

    @pl.when(t < n_cb)
    def _ctx():
        s = lax.dot_general(q_ref[0], kc, dn_t, preferred_element_type=jnp.float32) * scale
        m = jnp.max(s, axis=-1, keepdims=True)
        p = jnp.exp(s - m)
        den = jnp.sum(p, axis=-1, keepdims=True)
        o = jnp.dot(p.astype(vc.dtype), vc, preferred_element_type=jnp.float32) / den
        o_ref[0] = o.astype(o_ref.dtype)

    @pl.when(t >= n_cb)
    def _lat():
        r0 = (t - n_cb) * NA_ROWS_PER_STEP
        n_loc = NA_WIN_ROWS * GRID_W
        for i in range(NA_ROWS_PER_STEP):
            r = r0 + i
            rs = jnp.clip(r - NA_WIN_ROWS // 2, 0, rows - NA_WIN_ROWS)
            qi = q_ref[0, i * GRID_W:(i + 1) * GRID_W, :]
            start = pl.multiple_of(n_ctx + rs * GRID_W, GRID_W)
            kl = k_ref[0, pl.ds(start, n_loc), :]
            vl = v_ref[0, pl.ds(start, n_loc), :]
            s_loc = lax.dot_general(qi, kl, dn_t, preferred_element_type=jnp.float32) * scale
            s_loc = s_loc + bias_ref[0, r - rs]
            s_ctx = lax.dot_general(qi, kc, dn_t, preferred_element_type=jnp.float32) * scale
            m = jnp.maximum(jnp.max(s_loc, axis=-1, keepdims=True), jnp.max(s_ctx, axis=-1, keepdims=True))
            p_loc = jnp.exp(s_loc - m)
            p_ctx = jnp.exp(s_ctx - m)
            den = jnp.sum(p_loc, axis=-1, keepdims=True) + jnp.sum(p_ctx, axis=-1, keepdims=True)
            o = (jnp.dot(p_loc.astype(vl.dtype), vl, preferred_element_type=jnp.float32)
                 + jnp.dot(p_ctx.astype(vc.dtype), vc, preferred_element_type=jnp.float32)) / den
            o_ref[0, i * GRID_W:(i + 1) * GRID_W, :] = o.astype(o_ref.dtype)


def na_attention(z, bias, *, n_heads, head_dim, n_ctx, q_col, k_col, v_col, out_dtype=BF):
    B, ltot, _ = z.shape
    L = ltot - n_ctx
    rows = L // GRID_W
    tq = NA_ROWS_PER_STEP * GRID_W
    assert n_ctx % tq == 0 and L % tq == 0 and rows >= NA_WIN_ROWS
    body = functools.partial(_na_body, n_ctx=n_ctx, rows=rows, scale=head_dim ** -0.5)
    return pl.pallas_call(
        body,
        grid=(B, n_heads, ltot // tq),
        in_specs=[
            pl.BlockSpec((1, tq, head_dim), lambda b, h, t: (b, t, q_col + h)),
            pl.BlockSpec((1, ltot, head_dim), lambda b, h, t: (b, 0, k_col + h)),
            pl.BlockSpec((1, ltot, head_dim), lambda b, h, t: (b, 0, v_col + h)),
            pl.BlockSpec((1, NA_WIN_ROWS, GRID_W, NA_WIN_ROWS * GRID_W), lambda b, h, t: (h, 0, 0, 0)),
        ],
        out_specs=pl.BlockSpec((1, tq, head_dim), lambda b, h, t: (b, t, h)),
        out_shape=jax.ShapeDtypeStruct((B, ltot, n_heads * head_dim), out_dtype),
        compiler_params=_cparams(3),
        name="na_attention",
    )(z, z, z, bias)


def rope_tables(n_ctx, L, head_dim):
    n_freq = head_dim // 4
    freq = ROPE_BASE ** (-jnp.arange(n_freq, dtype=jnp.float32) / n_freq)
    t = jnp.arange(L)
    ang_r = (t // GRID_W).astype(jnp.float32)[:, None] * freq[None, :]
    ang_c = (t % GRID_W).astype(jnp.float32)[:, None] * freq[None, :]
    cos = jnp.concatenate([jnp.cos(ang_r), jnp.cos(ang_r), jnp.cos(ang_c), jnp.cos(ang_c)], axis=-1)
    sin = jnp.concatenate([-jnp.sin(ang_r), jnp.sin(ang_r), -jnp.sin(ang_c), jnp.sin(ang_c)], axis=-1)
    cos = jnp.concatenate([jnp.ones((n_ctx, head_dim), jnp.float32), cos], axis=0)
    sin = jnp.concatenate([jnp.zeros((n_ctx, head_dim), jnp.float32), sin], axis=0)
    return cos, sin


def _rope(x, cos, sin):
    d = x.shape[-1]
    quarter = d // 4
    lane = lax.broadcasted_iota(jnp.int32, x.shape, 1)
    first = (lane % (2 * quarter)) < quarter
    partner = jnp.where(first, pltpu.roll(x, d - quarter, 1), pltpu.roll(x, quarter, 1))
    return x * cos + partner * sin


def _ret_body(lg_ref, q_ref, k_ref, v_ref, cos_ref, sin_ref, *rest, reverse, k_scale):
    if reverse:
        of_ref, gate_ref, y_ref, s_ref = rest
    else:
        o_ref, s_ref = rest
    h = pl.program_id(1)
    c = pl.program_id(2)
    C = q_ref.shape[1]

    @pl.when(c == 0)
    def _init():
        s_ref[...] = jnp.zeros_like(s_ref)

    lg = lg_ref[h]
    cos = cos_ref[...]
    sin = sin_ref[...]
    q = _rope(q_ref[0].astype(jnp.float32), cos, sin)
    k = _rope(k_ref[0].astype(jnp.float32), cos, sin) * k_scale
    v = v_ref[0]
    pos = lax.broadcasted_iota(jnp.int32, (C, 1), 0).astype(jnp.float32)
    row = lax.broadcasted_iota(jnp.int32, (C, C), 0).astype(jnp.float32)
    colm = lax.broadcasted_iota(jnp.int32, (C, C), 1).astype(jnp.float32)
    if reverse:
        pos = (C - 1.0) - pos
        rel = colm - row
    else:
        rel = row - colm
    intra = jnp.where(rel >= 0, jnp.exp(lg * jnp.maximum(rel, 0.0)), 0.0)
    q_dec = jnp.exp(lg * (pos + 1.0))
    k_dec = jnp.exp(lg * ((C - 1.0) - pos))
    c_dec = jnp.exp(lg * C)
    scores = lax.dot_general(q.astype(BF), k.astype(BF), (((1,), (1,)), ((), ())),
                             preferred_element_type=jnp.float32) * intra
    s_old = s_ref[...]
    o = (jnp.dot(scores.astype(BF), v, preferred_element_type=jnp.float32)
         + jnp.dot((q * q_dec).astype(BF), s_old.astype(BF), preferred_element_type=jnp.float32))
    s_ref[...] = c_dec * s_old + lax.dot_general((k * k_dec).astype(BF), v, (((0,), (0,)), ((), ())),
                                                 preferred_element_type=jnp.float32)
    if reverse:
        o = o + of_ref[0]
        mu = jnp.mean(o, axis=-1, keepdims=True)
        oc = o - mu
        var = jnp.mean(oc * oc, axis=-1, keepdims=True)
        g = gate_ref[0].astype(jnp.float32)
        y_ref[0] = (oc * lax.rsqrt(var + LN_EPS) * (g * jax.nn.sigmoid(g))).astype(y_ref.dtype)
    else:
        o_ref[0] = o


def retention(z, cos, sin, log_dec_f, log_dec_b, *, n_heads, qk_dim, v_dim, n_ctx,
              q_col, k_col, v_col, g_col, out_dtype=BF):
    B, ltot, _ = z.shape
    C = RET_TILE
    assert n_ctx % C == 0 and ltot % C == 0
    n_c = ltot // C
    n_cc = n_ctx // C
    k_scale = qk_dim ** -0.5

    def fwd_idx(c):
        return c

    def bwd_idx(c):
        return jnp.where(c < n_cc, n_cc - 1 - c, n_c - 1 - (c - n_cc))

    def specs(cidx):
        return [
            pl.BlockSpec((1, C, qk_dim), lambda b, h, c, lg: (b, cidx(c), q_col + h)),
            pl.BlockSpec((1, C, qk_dim), lambda b, h, c, lg: (b, cidx(c), k_col + h)),
            pl.BlockSpec((1, C, v_dim), lambda b, h, c, lg: (b, cidx(c), v_col + h)),
            pl.BlockSpec((C, qk_dim), lambda b, h, c, lg: (cidx(c), 0)),
            pl.BlockSpec((C, qk_dim), lambda b, h, c, lg: (cidx(c), 0)),
        ]

    o_f = pl.pallas_call(
        functools.partial(_ret_body, reverse=False, k_scale=k_scale),
        grid_spec=pltpu.PrefetchScalarGridSpec(
            num_scalar_prefetch=1, grid=(B, n_heads, n_c),
            in_specs=specs(fwd_idx),
            out_specs=pl.BlockSpec((1, C, v_dim), lambda b, h, c, lg: (b, c, h)),
            scratch_shapes=[pltpu.VMEM((qk_dim, v_dim), jnp.float32)]),
        out_shape=jax.ShapeDtypeStruct((B, ltot, n_heads * v_dim), jnp.float32),
        compiler_params=_cparams(3), name="retention_fwd",
    )(log_dec_f.astype(jnp.float32), z, z, z, cos, sin)

    return pl.pallas_call(
        functools.partial(_ret_body, reverse=True, k_scale=k_scale),
        grid_spec=pltpu.PrefetchScalarGridSpec(
            num_scalar_prefetch=1, grid=(B, n_heads, n_c),
            in_specs=specs(bwd_idx) + [
                pl.BlockSpec((1, C, v_dim), lambda b, h, c, lg: (b, bwd_idx(c), h)),
                pl.BlockSpec((1, C, v_dim), lambda b, h, c, lg: (b, bwd_idx(c), g_col + h)),
            ],
            out_specs=pl.BlockSpec((1, C, v_dim), lambda b, h, c, lg: (b, bwd_idx(c), h)),
            scratch_shapes=[pltpu.VMEM((qk_dim, v_dim), jnp.float32)]),
        out_shape=jax.ShapeDtypeStruct((B, ltot, n_heads * v_dim), out_dtype),
        compiler_params=_cparams(3), name="retention_bwd",
    )(log_dec_b.astype(jnp.float32), z, z, z, cos, sin, o_f, z)


RET_CHUNK = 128


def _head_norm(x):
    mu = x.mean(-1, keepdims=True)
    var = jnp.square(x - mu).mean(-1, keepdims=True)
    return (x - mu) * lax.rsqrt(var + LN_EPS)


def _axial_rope(x, pos_row, pos_col):
    half = x.shape[-1] // 2
    n_freq = half // 2
    freq = ROPE_BASE ** (-jnp.arange(n_freq, dtype=jnp.float32) / n_freq)

    def rotate(xp, pos):
        ang = pos.astype(jnp.float32)[:, None] * freq[None, :]
        cos = jnp.cos(ang)[None, :, None, :]
        sin = jnp.sin(ang)[None, :, None, :]
        x1, x2 = xp[..., :n_freq], xp[..., n_freq:]
        return jnp.concatenate([x1 * cos - x2 * sin, x1 * sin + x2 * cos], axis=-1)

    return jnp.concatenate([rotate(x[..., :half], pos_row), rotate(x[..., half:], pos_col)], axis=-1)


def _retention_chunkwise(q, k, v, log_gamma, s0, with_output):
    B, L, H, _ = q.shape
    dv = v.shape[-1]
    n = L // RET_CHUNK
    idx = jnp.arange(RET_CHUNK, dtype=jnp.float32)
    rel = idx[:, None] - idx[None, :]
    intra = jnp.where(rel >= 0, jnp.exp(log_gamma[:, None, None] * jnp.maximum(rel, 0.0)), 0.0)
    q_dec = jnp.exp(log_gamma[None, :] * (idx[:, None] + 1.0))[None, :, :, None]
    k_dec = jnp.exp(log_gamma[None, :] * (RET_CHUNK - 1.0 - idx[:, None]))[None, :, :, None]
    c_dec = jnp.exp(log_gamma * RET_CHUNK)[None, :, None, None]

    def blocks(z):
        return z.reshape(B, n, RET_CHUNK, H, z.shape[-1]).swapaxes(0, 1)

    def step(s, qkv):
        qi, ki, vi = qkv
        s_new = c_dec * s + jnp.einsum('bchd,bche->bhde', ki * k_dec, vi)
        if not with_output:
            return s_new, None
        scores = jnp.einsum('bqhd,bkhd->bhqk', qi, ki) * intra[None]
        o = (jnp.einsum('bhqk,bkhe->bqhe', scores, vi)
             + jnp.einsum('bqhd,bhde->bqhe', qi * q_dec, s))
        return s_new, o

    s_fin, o = lax.scan(step, s0, (blocks(q), blocks(k), blocks(v)))
    if with_output:
        o = o.swapaxes(0, 1).reshape(B, L, H, dv)
    return o, s_fin


def _bidirectional_retention(q, k, v, qc, kc, vc, log_dec_f, log_dec_b):
    s0 = jnp.zeros((q.shape[0], RET_HEADS, RET_QK_DIM, RET_V_DIM), jnp.float32)
    lf = log_dec_f.astype(jnp.float32)
    lb = log_dec_b.astype(jnp.float32)

    def flip(t):
        return t[:, ::-1]

    oc_f, s_cf = _retention_chunkwise(qc, kc, vc, lf, s0, True)
    o_f, _ = _retention_chunkwise(q, k, v, lf, s_cf, True)
    oc_b, s_cb = _retention_chunkwise(flip(qc), flip(kc), flip(vc), lb, s0, True)
    o_b, _ = _retention_chunkwise(flip(q), flip(k), flip(v), lb, s_cb, True)
    return o_f + flip(o_b), oc_f + flip(oc_b)


def retention_jax(z, log_dec_f, log_dec_b, n_ctx):
    B, ltot, _ = z.shape
    L = ltot - n_ctx
    zf = z.astype(jnp.float32)
    base = 3 * NA_WIDTH
    sp = [base, base + RET_QK_WIDTH, base + 2 * RET_QK_WIDTH, base + 2 * RET_QK_WIDTH + RET_V_WIDTH,
          base + 2 * RET_QK_WIDTH + 2 * RET_V_WIDTH]
    zl = [zf[:, n_ctx:, sp[i]:sp[i + 1]] for i in range(4)]
    zc = [zf[:, :n_ctx, sp[i]:sp[i + 1]] for i in range(4)]

    def heads(t):
        return t.reshape(B, t.shape[1], RET_HEADS, -1)

    t = jnp.arange(L)
    pos_row = t // GRID_W
    pos_col = t % GRID_W
    k_scale = RET_QK_DIM ** -0.5
    qb = _axial_rope(heads(zl[0]), pos_row, pos_col)
    kb = _axial_rope(heads(zl[1]), pos_row, pos_col) * k_scale
    o, oc = _bidirectional_retention(qb, kb, heads(zl[2]), heads(zc[0]), heads(zc[1]) * k_scale, heads(zc[2]),
                                     log_dec_f, log_dec_b)

    def merge(o_b, gate):
        return _head_norm(o_b).reshape(B, o_b.shape[1], RET_V_WIDTH) * jax.nn.silu(gate)

    return jnp.concatenate([merge(oc, zc[3]), merge(o, zl[3])], axis=1).astype(BF)


def _s5_discretize(lam_re, lam_im, log_dt, b_re, b_im):
    dt = jnp.exp(log_dt)[:, None]
    logmag = lam_re * dt
    angle = lam_im * dt
    nr = jnp.exp(logmag) * jnp.cos(angle) - 1.0
    ni = jnp.exp(logmag) * jnp.sin(angle)
    den = lam_re * lam_re + lam_im * lam_im
    fr = ((nr * lam_re + ni * lam_im) / den)[..., None]
    fi = ((ni * lam_re - nr * lam_im) / den)[..., None]
    return logmag, angle, fr * b_re - fi * b_im, fr * b_im + fi * b_re


def s5_operators(lam_re, lam_im, log_dt, b_re, b_im, c_re, c_im):
    T, CH, P = S5_T, S5_GROUP_CH, S5_STATE
    G = lam_re.shape[1]
    tau = jnp.arange(T + 1, dtype=jnp.float32)[:, None, None]
    t_i = jnp.arange(T)
    kt = 0.0
    wbs, wcs, a_rows = [], [], []
    for d in range(2):
        logmag, angle, bb_re, bb_im = _s5_discretize(lam_re[d], lam_im[d], log_dt[d], b_re, b_im)
        mag = jnp.exp(logmag[None] * tau)
        pr = mag * jnp.cos(angle[None] * tau)
        pi = mag * jnp.sin(angle[None] * tau)
        cr, ci = c_re[d], c_im[d]
        z_re = pr[..., None] * bb_re[None] - pi[..., None] * bb_im[None]
        z_im = pr[..., None] * bb_im[None] + pi[..., None] * bb_re[None]
        m = (jnp.einsum('ghp,tgpk->tghk', cr, z_re, precision=HI)
             - jnp.einsum('ghp,tgpk->tghk', ci, z_im, precision=HI))
        lag = (t_i[:, None] - t_i[None, :]) if d == 0 else (t_i[None, :] - t_i[:, None])
        blk = jnp.where((lag >= 0)[:, :, None, None, None], m[jnp.clip(lag, 0, T)], 0.0)
        kt = kt + blk.transpose(2, 1, 4, 0, 3).reshape(G, T * CH, T * CH)
        dist = (T - 1 - t_i) if d == 0 else t_i
        wbs += [z_re[dist].transpose(1, 0, 3, 2).reshape(G, T * CH, P),
                z_im[dist].transpose(1, 0, 3, 2).reshape(G, T * CH, P)]
        n_t = (t_i + 1) if d == 0 else (T - t_i)
        qr = pr[n_t][:, :, None, :]
        qi = pi[n_t][:, :, None, :]
        wcs += [(cr[None] * qr - ci[None] * qi).transpose(1, 3, 0, 2).reshape(G, P, T * CH),
                (-(cr[None] * qi + ci[None] * qr)).transpose(1, 3, 0, 2).reshape(G, P, T * CH)]
        a_rows += [pr[T].reshape(-1), pi[T].reshape(-1)]

    def pair_diag(w):
        g2 = w.reshape(G // 2, 2, w.shape[1], w.shape[2])
        z = jnp.zeros_like(g2[:, 0])
        return jnp.concatenate([jnp.concatenate([g2[:, 0], z], axis=2),
                                jnp.concatenate([z, g2[:, 1]], axis=2)], axis=1)

    wb = jnp.concatenate([pair_diag(w) for w in wbs], axis=2)
    wc = jnp.concatenate([pair_diag(w) for w in wcs], axis=1)
    return kt.astype(BF), wb.astype(BF), wc.astype(BF), jnp.stack(a_rows)


def _s5_body(u_ref, kt_ref, wb_ref, wc_ref, a_ref, d_ref, y_ref, e_ref, x_ref, ug_ref, yg_ref, *, n_c, n_cc):
    T, CH, P2, W, GB = S5_T, S5_GROUP_CH, 2 * S5_STATE, S5_T * S5_GROUP_CH, S5_TILE_GROUPS
    n_pair = GB // 2
    half = LANES // CH
    lane = lax.broadcasted_iota(jnp.int32, (n_c, LANES), 1)

    def u_step(s):
        return u_ref[0, pl.ds(s, n_c, stride=T), :]

    def place(x, src_slot, dst_slot):
        shift = (CH * (dst_slot - src_slot)) % LANES
        r = pltpu.roll(x, shift, 1) if shift else x
        return jnp.where((lane >= CH * dst_slot) & (lane < CH * (dst_slot + 1)), r, 0.0)

    for g in range(GB):
        for k in range(T // half):
            acc = place(u_step(half * k), g, 0)
            for s8 in range(1, half):
                acc = acc + place(u_step(half * k + s8), g, s8)
            ug_ref[:, W * g + LANES * k:W * g + LANES * (k + 1)] = acc.astype(BF)
    for q in range(n_pair):
        e = jnp.dot(ug_ref[:, 2 * W * q:2 * W * (q + 1)], wb_ref[q], preferred_element_type=jnp.float32)
        for comp in range(4):
            e_ref[comp, :, q * P2:(q + 1) * P2] = e[:, comp * P2:(comp + 1) * P2]
    a_fr, a_fi, a_br, a_bi = a_ref[0:1, :], a_ref[1:2, :], a_ref[2:3, :], a_ref[3:4, :]

    def step(i, carry):
        xfr, xfi, xbr, xbi = carry
        j = jnp.where(i < n_cc, n_cc - 1 - i, n_c - 1 - (i - n_cc))
        x_ref[0, pl.ds(i, 1), :] = xfr
        x_ref[1, pl.ds(i, 1), :] = xfi
        x_ref[2, pl.ds(j, 1), :] = xbr
        x_ref[3, pl.ds(j, 1), :] = xbi
        return (a_fr * xfr - a_fi * xfi + e_ref[0, pl.ds(i, 1), :],
                a_fr * xfi + a_fi * xfr + e_ref[1, pl.ds(i, 1), :],
                a_br * xbr - a_bi * xbi + e_ref[2, pl.ds(j, 1), :],
                a_br * xbi + a_bi * xbr + e_ref[3, pl.ds(j, 1), :])

    zero = jnp.zeros((1, n_pair * P2), jnp.float32)
    lax.fori_loop(0, n_c, step, (zero, zero, zero, zero))
    for q in range(n_pair):
        xcat = jnp.concatenate([x_ref[comp, :, q * P2:(q + 1) * P2] for comp in range(4)], axis=1).astype(BF)
        yx = jnp.dot(xcat, wc_ref[q], preferred_element_type=jnp.float32)
        for gi in range(2):
            g = 2 * q + gi
            yg_ref[:, W * g:W * (g + 1)] = (
                jnp.dot(ug_ref[:, W * g:W * (g + 1)], kt_ref[g], preferred_element_type=jnp.float32)
                + yx[:, gi * W:(gi + 1) * W])
    d = d_ref[...]
    for s in range(T):
        k, s8 = divmod(s, half)
        acc = place(yg_ref[:, LANES * k:LANES * (k + 1)], s8, 0)
        for g in range(1, GB):
            acc = acc + place(yg_ref[:, W * g + LANES * k:W * g + LANES * (k + 1)], s8, g)
        y_ref[0, pl.ds(s, n_c, stride=T), :] = jax.nn.gelu(acc + d * u_step(s))


def s5_scan_gelu(u, ops, d_skip, *, n_ctx):
    kt, wb, wc, a = ops
    B, ltot, D = u.shape
    T, CH, P, GB = S5_T, S5_GROUP_CH, S5_STATE, S5_TILE_GROUPS
    W = T * CH
    n_c = ltot // T
    body = functools.partial(_s5_body, n_c=n_c, n_cc=n_ctx // T)
    tile = pl.BlockSpec((1, ltot, LANES), lambda j, b: (b, 0, j))
    return pl.pallas_call(
        body,
        grid=(D // LANES, B),
        in_specs=[
            tile,
            pl.BlockSpec((GB, W, W), lambda j, b: (j, 0, 0)),
            pl.BlockSpec((GB // 2, 2 * W, 8 * P), lambda j, b: (j, 0, 0)),
            pl.BlockSpec((GB // 2, 8 * P, 2 * W), lambda j, b: (j, 0, 0)),
            pl.BlockSpec((4, GB * P), lambda j, b: (0, j)),
            pl.BlockSpec((1, LANES), lambda j, b: (0, j)),
        ],
        out_specs=tile,
        out_shape=jax.ShapeDtypeStruct((B, ltot, D), jnp.float32),
        scratch_shapes=[pltpu.VMEM((4, n_c, GB * P), jnp.float32),
                        pltpu.VMEM((4, n_c, GB * P), jnp.float32),
                        pltpu.VMEM((n_c, GB * W), BF),
                        pltpu.VMEM((n_c, GB * W), jnp.float32)],
        compiler_params=_cparams(2),
        name="s5_scan",
    )(u, kt, wb, wc, a, d_skip.reshape(1, D))


ROUTER_LANES = 128


def _router_body(h_ref, w_ref, b_ref, o_ref):
    logits = jnp.dot(h_ref[...], w_ref[...], preferred_element_type=jnp.float32) + b_ref[...]
    lane = lax.broadcasted_iota(jnp.int32, logits.shape, 1)

    def first_argmax(vals, vmax):
        return jnp.min(jnp.where(vals == vmax, lane, ROUTER_LANES), axis=-1, keepdims=True)

    gl = jnp.where(lane < MOE_GROUPS, logits, NEG_BIG)
    gmax = jnp.max(gl, axis=-1, keepdims=True)
    gsum = jnp.sum(jnp.where(lane < MOE_GROUPS, jnp.exp(gl - gmax), 0.0), axis=-1, keepdims=True)
    g_w = 1.0 / gsum
    lo = MOE_GROUPS + MOE_EXPERTS_PER_GROUP * first_argmax(gl, gmax)
    el = jnp.where((lane >= lo) & (lane < lo + MOE_EXPERTS_PER_GROUP), logits, NEG_BIG)
    m1 = jnp.max(el, axis=-1, keepdims=True)
    i1 = first_argmax(el, m1)
    el2 = jnp.where(lane == i1, NEG_BIG, el)
    m2 = jnp.max(el2, axis=-1, keepdims=True)
    i2 = first_argmax(el2, m2)
    e2 = jnp.exp(m2 - m1)
    den = 1.0 + e2
    o_ref[...] = jnp.where(lane == 0, (i1 - MOE_GROUPS).astype(jnp.float32),
                           jnp.where(lane == 1, (i2 - MOE_GROUPS).astype(jnp.float32),
                                     jnp.where(lane == 2, g_w * (1.0 / den),
                                               jnp.where(lane == 3, g_w * (e2 / den), 0.0))))


def moe_router(h, w_group, b_group, w_expert, b_expert):
    n, d = h.shape
    n_log = MOE_GROUPS + MOE_EXPERTS
    w = jnp.zeros((d, ROUTER_LANES), jnp.float32).at[:, :MOE_GROUPS].set(w_group).at[:, MOE_GROUPS:n_log].set(w_expert)
    b = jnp.zeros((1, ROUTER_LANES), jnp.float32).at[0, :MOE_GROUPS].set(b_group).at[0, MOE_GROUPS:n_log].set(b_expert)
    tm = MM_TILE_M if n % MM_TILE_M == 0 else n
    route = pl.pallas_call(
        _router_body,
        grid=(n // tm,),
        in_specs=[pl.BlockSpec((tm, d), lambda i: (i, 0)),
                  pl.BlockSpec((d, ROUTER_LANES), lambda i: (0, 0)),
                  pl.BlockSpec((1, ROUTER_LANES), lambda i: (0, 0))],
        out_specs=pl.BlockSpec((tm, ROUTER_LANES), lambda i: (i, 0)),
        out_shape=jax.ShapeDtypeStruct((n, ROUTER_LANES), jnp.float32),
        compiler_params=_cparams(1),
        name="moe_router",
    )(h, w.astype(BF), b)
    return route[:, 0:2].astype(jnp.int32), route[:, 2:4]


def _expert_body(blk_e_ref, blk_on_ref, x_ref, wg_ref, wu_ref, wd_ref, y_ref):
    i = pl.program_id(0)

    @pl.when(blk_on_ref[i] != 0)
    def _compute():
        x = x_ref[...]
        g = jnp.dot(x, wg_ref[0], preferred_element_type=jnp.float32)
        u = jnp.dot(x, wu_ref[0], preferred_element_type=jnp.float32)
        hid = (g * jax.nn.sigmoid(g) * u).astype(BF)
        y_ref[...] = jnp.dot(hid, wd_ref[0], preferred_element_type=jnp.float32).astype(y_ref.dtype)

    @pl.when(blk_on_ref[i] == 0)
    def _unused_block():
        y_ref[...] = jnp.zeros_like(y_ref)


def expert_mlp(xb, blk_e, blk_on, w_gate, w_up, w_down, out_dtype=jnp.float32):
    n_rows, d = xb.shape
    hid = w_gate.shape[2]
    return pl.pallas_call(
        _expert_body,
        grid_spec=pltpu.PrefetchScalarGridSpec(
            num_scalar_prefetch=2, grid=(n_rows // EXPERT_BLOCK,),
            in_specs=[pl.BlockSpec((EXPERT_BLOCK, d), lambda i, be, on: (i, 0)),
                      pl.BlockSpec((1, d, hid), lambda i, be, on: (be[i], 0, 0)),
                      pl.BlockSpec((1, d, hid), lambda i, be, on: (be[i], 0, 0)),
                      pl.BlockSpec((1, hid, d), lambda i, be, on: (be[i], 0, 0))],
            out_specs=pl.BlockSpec((EXPERT_BLOCK, d), lambda i, be, on: (i, 0))),
        out_shape=jax.ShapeDtypeStruct((n_rows, d), out_dtype),
        compiler_params=_cparams(1),
        name="expert_mlp",
    )(blk_e, blk_on, xb, w_gate, w_up, w_down)


def moe_dispatch(eid, n_tok):
    n_asg = n_tok * MOE_TOP_K
    flat_e = eid.reshape(-1)
    order = jnp.argsort(flat_e)
    e_sorted = flat_e[order]
    counts = jnp.sum(flat_e[:, None] == jnp.arange(MOE_EXPERTS)[None, :], axis=0).astype(jnp.int32)
    padded = (counts + EXPERT_BLOCK - 1) // EXPERT_BLOCK * EXPERT_BLOCK
    pad_end = jnp.cumsum(padded)
    pad_start = pad_end - padded
    start = jnp.cumsum(counts) - counts
    dest_sorted = pad_start[e_sorted] + jnp.arange(n_asg, dtype=jnp.int32) - start[e_sorted]
    n_blocks = (n_asg + MOE_EXPERTS * (EXPERT_BLOCK - 1)) // EXPERT_BLOCK + 1
    n_rows = n_blocks * EXPERT_BLOCK
    blk_first = jnp.arange(n_blocks, dtype=jnp.int32) * EXPERT_BLOCK
    blk_e = jnp.minimum(jnp.sum(blk_first[:, None] >= pad_end[None, :], axis=1), MOE_EXPERTS - 1).astype(jnp.int32)
    blk_on = (blk_first < pad_end[-1]).astype(jnp.int32)
    row = jnp.arange(n_rows, dtype=jnp.int32)
    row_e = jnp.repeat(blk_e, EXPERT_BLOCK)
    pos = row - pad_start[row_e]
    valid = (pos < counts[row_e]) & (row < pad_end[-1])
    src = jnp.clip(start[row_e] + pos, 0, n_asg - 1)
    row_tok = jnp.where(valid, order[src] // MOE_TOP_K, n_tok).astype(jnp.int32)
    dest = dest_sorted[jnp.argsort(order)].reshape(n_tok, MOE_TOP_K)
    return row_tok, dest, blk_e, blk_on


def hier_moe(h, w_group, b_group, w_expert, b_expert, w_gate, w_up, w_down):
    n_tok, d = h.shape
    eid, wts = moe_router(h, w_group, b_group, w_expert, b_expert)
    row_tok, dest, blk_e, blk_on = moe_dispatch(eid, n_tok)
    h_pad = jnp.concatenate([h, jnp.zeros((1, d), h.dtype)], axis=0)
    yb = expert_mlp(h_pad[row_tok], blk_e, blk_on, w_gate.astype(BF), w_up.astype(BF), w_down.astype(BF))
    return yb[dest[:, 0]] * wts[:, 0:1] + yb[dest[:, 1]] * wts[:, 1:2]


def kernel(x, c, ctx, c_ctx, mod_w, mod_b, ln_g, ln_b, mix_w_in, mix_w_out, na_rpb,
           ret_log_decay_fwd, ret_log_decay_bwd, s5_w_in, s5_lam_re, s5_lam_im, s5_log_dt,
           s5_b_re, s5_b_im, s5_c_re, s5_c_im, s5_d, s5_w_glu, s5_w_out,
           moe_w_group, moe_b_group, moe_w_expert, moe_b_expert, moe_w_gate, moe_w_up, moe_w_down):
    B, L, D = x.shape
    n_ctx = ctx.shape[1]
    ltot = n_ctx + L
    n_tok = B * ltot
    xa = jnp.concatenate([ctx, x], axis=1)
    cond = jnp.concatenate([jax.nn.silu(c), jax.nn.silu(c_ctx)[None]], axis=0)
    mods = []
    for layer in range(DEPTH):
        m = (cond @ mod_w[layer] + mod_b[layer]).reshape(B + 1, 6, 1, D)
        mods.append([m[:, i] for i in range(6)])
    cos, sin = rope_tables(n_ctx, L, RET_QK_DIM)

    h = modulate(xa, mods[0][0], mods[0][1], n_ctx=n_ctx)
    for layer in range(DEPTH):
        last = layer == DEPTH - 1
        m = mods[layer]
        j = layer // 2
        h2d = h.reshape(n_tok, D)
        if layer % 2 == 0:
            z = _mm(h2d, mix_w_in[j].astype(BF), BF).reshape(B, ltot, -1)
            o_na = na_attention(z, na_bias_table(na_rpb[j]), n_heads=NA_HEADS, head_dim=NA_HEAD_DIM,
                                n_ctx=n_ctx, q_col=0, k_col=NA_HEADS, v_col=2 * NA_HEADS)
            y_ret = retention_jax(z, ret_log_decay_fwd[j], ret_log_decay_bwd[j], n_ctx)
            y = jnp.concatenate([o_na, y_ret], axis=-1).reshape(n_tok, D)
            o = _mm(y, mix_w_out[j].astype(BF))
        else:
            u = _mm(h2d, s5_w_in[j].astype(BF)).reshape(B, ltot, D)
            ops = s5_operators(s5_lam_re[j], s5_lam_im[j], s5_log_dt[j], s5_b_re[j], s5_b_im[j],
                               s5_c_re[j], s5_c_im[j])
            g = s5_scan_gelu(u, ops, s5_d[j], n_ctx=n_ctx).astype(BF).reshape(n_tok, D)
            gg = _mm(g, s5_w_glu[j].astype(BF), BF, glu_gate=g)
            o = _mm(gg, s5_w_out[j].astype(BF))
        x1, h_moe = residual_ln(xa, o.reshape(B, ltot, D), m[2], ln_g[layer, 0], ln_b[layer, 0],
                                n_ctx=n_ctx, next_shift=m[3], next_scale=m[4])
        f = hier_moe(h_moe.reshape(n_tok, D),
                     moe_w_group[layer], moe_b_group[layer], moe_w_expert[layer], moe_b_expert[layer],
                     moe_w_gate[layer], moe_w_up[layer], moe_w_down[layer]).reshape(B, ltot, D)
        if last:
            return residual_ln(x1, f, m[5], ln_g[layer, 1], ln_b[layer, 1], n_ctx=n_ctx, latent_only=True)
        xa, h = residual_ln(x1, f, m[5], ln_g[layer, 1], ln_b[layer, 1], n_ctx=n_ctx,
                            next_shift=mods[layer + 1][0], next_scale=mods[layer + 1][1])
```

```python
import functools

import jax
import jax.numpy as jnp
from jax import lax
from jax.experimental import pallas as pl
from jax.experimental.pallas import tpu as pltpu

D_MODEL = 4096
DEPTH = 2
GRID_W = 64
NA_HEAD_DIM = 128
NA_HEADS = D_MODEL // 2 // NA_HEAD_DIM
NA_WIDTH = NA_HEADS * NA_HEAD_DIM
NA_WIN_ROWS = 8
NA_WIN_COLS = 16
RET_HEADS = 8
RET_V_DIM = D_MODEL // 2 // RET_HEADS
RET_QK_DIM = RET_V_DIM // 2
RET_QK_WIDTH = RET_HEADS * RET_QK_DIM
RET_V_WIDTH = RET_HEADS * RET_V_DIM
S5_GROUP_CH = 16
S5_STATE = 64
S5_T = 16
LANES = 128
S5_TILE_GROUPS = LANES // S5_GROUP_CH
MOE_GROUPS = 4
MOE_EXPERTS_PER_GROUP = 8
MOE_EXPERTS = MOE_GROUPS * MOE_EXPERTS_PER_GROUP
MOE_TOP_K = 2
EXPERT_BLOCK = 256
ROPE_BASE = 10000.0
LN_EPS = 1e-5
DEEPNORM_ALPHA = (2.0 * DEPTH) ** 0.25
NEG_BIG = -1e30

VMEM_LIMIT_BYTES = 48 * 1024 * 1024
S5_VMEM_LIMIT_BYTES = 56 * 1024 * 1024
MM_TILE_M = 512
MM_TILE_N = 512
ROW_TILE = 256
RET_TILE = 256
NA_ROWS_PER_STEP = 4
HI = lax.Precision.HIGHEST
BF = jnp.bfloat16


def _cparams(n_axes, vmem_limit_bytes=VMEM_LIMIT_BYTES):
    return pltpu.CompilerParams(dimension_semantics=("arbitrary",) * n_axes,
                                vmem_limit_bytes=vmem_limit_bytes)


def _mm_body(x_ref, w_ref, o_ref):
    o_ref[...] = jnp.dot(x_ref[...], w_ref[...],
                         preferred_element_type=jnp.float32).astype(o_ref.dtype)


def _mm_glu_body(x_ref, w_ref, g_ref, o_ref):
    t = jnp.dot(x_ref[...], w_ref[...], preferred_element_type=jnp.float32)
    o_ref[...] = (g_ref[...].astype(jnp.float32) * jax.nn.sigmoid(t)).astype(o_ref.dtype)


def _mm(x, w, out_dtype=jnp.float32, glu_gate=None):
    m, k = x.shape
    n = w.shape[1]
    tm = MM_TILE_M if m % MM_TILE_M == 0 else m
    tn = MM_TILE_N if n % MM_TILE_N == 0 else n
    in_specs = [pl.BlockSpec((tm, k), lambda i, j: (i, 0)),
                pl.BlockSpec((k, tn), lambda i, j: (0, j))]
    args = [x, w]
    body = _mm_body
    if glu_gate is not None:
        in_specs.append(pl.BlockSpec((tm, tn), lambda i, j: (i, j)))
        args.append(glu_gate)
        body = _mm_glu_body
    return pl.pallas_call(
        body,
        grid=(m // tm, n // tn),
        in_specs=in_specs,
        out_specs=pl.BlockSpec((tm, tn), lambda i, j: (i, j)),
        out_shape=jax.ShapeDtypeStruct((m, n), out_dtype),
        compiler_params=_cparams(2),
        name="dense_mm",
    )(*args)


def _mod_row_spec(n_ctx_blocks, n_batch):
    return pl.BlockSpec((1, 1, D_MODEL), lambda b, t: (jnp.where(t < n_ctx_blocks, n_batch, b), 0, 0))


def _modulate_body(x_ref, shift_ref, scale_ref, h_ref):
    h_ref[0] = (x_ref[0] * (1.0 + scale_ref[0]) + shift_ref[0]).astype(h_ref.dtype)


def modulate(x, shift, scale, *, n_ctx):
    B, n, d = x.shape
    blk = pl.BlockSpec((1, ROW_TILE, d), lambda b, t: (b, t, 0))
    mod = _mod_row_spec(n_ctx // ROW_TILE, B)
    return pl.pallas_call(
        _modulate_body,
        grid=(B, n // ROW_TILE),
        in_specs=[blk, mod, mod],
        out_specs=blk,
        out_shape=jax.ShapeDtypeStruct((B, n, d), BF),
        compiler_params=_cparams(2),
        name="modulate",
    )(x, shift, scale)


def _ln_body(x_ref, o_ref, gate_ref, g_ref, b_ref, *rest, with_next):
    if with_next:
        shift_ref, scale_ref, x1_ref, h_ref = rest
    else:
        (x1_ref,) = rest
    y = DEEPNORM_ALPHA * x_ref[0] + gate_ref[0] * o_ref[0].astype(jnp.float32)
    mu = jnp.mean(y, axis=-1, keepdims=True)
    yc = y - mu
    var = jnp.mean(yc * yc, axis=-1, keepdims=True)
    x1 = yc * lax.rsqrt(var + LN_EPS) * g_ref[...] + b_ref[...]
    x1_ref[0] = x1
    if with_next:
        h_ref[0] = (x1 * (1.0 + scale_ref[0]) + shift_ref[0]).astype(h_ref.dtype)


def residual_ln(x, o, gate, ln_g, ln_b, *, n_ctx, next_shift=None, next_scale=None, latent_only=False):
    B, n, d = x.shape
    n_cb = n_ctx // ROW_TILE
    blk = pl.BlockSpec((1, ROW_TILE, d), lambda b, t: (b, t, 0))
    mod = _mod_row_spec(n_cb, B)
    vec = pl.BlockSpec((1, d), lambda b, t: (0, 0))
    with_next = next_shift is not None
    in_specs = [blk, blk, mod, vec, vec]
    args = [x, o, gate, ln_g.reshape(1, d), ln_b.reshape(1, d)]
    if latent_only:
        x1_spec = pl.BlockSpec((1, ROW_TILE, d), lambda b, t: (b, jnp.maximum(t - n_cb, 0), 0))
        x1_shape = jax.ShapeDtypeStruct((B, n - n_ctx, d), jnp.float32)
    else:
        x1_spec = blk
        x1_shape = jax.ShapeDtypeStruct((B, n, d), jnp.float32)
    out_specs, out_shape = x1_spec, x1_shape
    if with_next:
        in_specs += [mod, mod]
        args += [next_shift, next_scale]
        out_specs = [x1_spec, blk]
        out_shape = [x1_shape, jax.ShapeDtypeStruct((B, n, d), BF)]
    return pl.pallas_call(
        functools.partial(_ln_body, with_next=with_next),
        grid=(B, n // ROW_TILE),
        in_specs=in_specs,
        out_specs=out_specs,
        out_shape=out_shape,
        compiler_params=_cparams(2),
        name="residual_ln",
    )(*args)


def na_bias_table(rpb):
    H = rpb.shape[0]
    col = jnp.arange(GRID_W)
    col_start = jnp.clip(col - NA_WIN_COLS // 2, 0, GRID_W - NA_WIN_COLS)
    col_ok = (col[None, :] >= col_start[:, None]) & (col[None, :] < col_start[:, None] + NA_WIN_COLS)
    col_idx = jnp.clip(col[None, :] - col[:, None] + NA_WIN_COLS - 1, 0, 2 * NA_WIN_COLS - 2)
    rpb_col = rpb[:, :, col_idx]
    offs = jnp.arange(NA_WIN_ROWS)
    j = jnp.arange(NA_WIN_ROWS)
    row_idx = j[None, :] - offs[:, None] + (NA_WIN_ROWS - 1)
    b = rpb_col[:, row_idx]
    b = jnp.where(col_ok[None, None, None], b, NEG_BIG)
    b = b.transpose(0, 1, 3, 2, 4).reshape(H, NA_WIN_ROWS, GRID_W, NA_WIN_ROWS * GRID_W)
    return b.astype(jnp.float32)


def _na_body(q_ref, k_ref, v_ref, bias_ref, o_ref, *, n_ctx, rows, scale):
    t = pl.program_id(2)
    n_cb = n_ctx // (NA_ROWS_PER_STEP * GRID_W)
    kc = k_ref[0, 0:n_ctx, :]
    vc = v_ref[0, 0:n_ctx, :]
    dn_t = (((1,), (1,)), ((), ()))

    @pl.when(t < n_cb)
    def _ctx():
        s = lax.dot_general(q_ref[0], kc, dn_t, preferred_element_type=jnp.float32) * scale
        m = jnp.max(s, axis=-1, keepdims=True)
        p = jnp.exp(s - m)
        den = jnp.sum(p, axis=-1, keepdims=True)
        o = jnp.dot(p.astype(vc.dtype), vc, preferred_element_type=jnp.float32) / den
        o_ref[0] = o.astype(o_ref.dtype)

    @pl.when(t >= n_cb)
    def _lat():
        r0 = (t - n_cb) * NA_ROWS_PER_STEP
        n_loc = NA_WIN_ROWS * GRID_W
        for i in range(NA_ROWS_PER_STEP):
            r = r0 + i
            rs = jnp.clip(r - NA_WIN_ROWS // 2, 0, rows - NA_WIN_ROWS)
            qi = q_ref[0, i * GRID_W:(i + 1) * GRID_W, :]
            start = pl.multiple_of(n_ctx + rs * GRID_W, GRID_W)
            kl = k_ref[0, pl.ds(start, n_loc), :]
            vl = v_ref[0, pl.ds(start, n_loc), :]
            s_loc = lax.dot_general(qi, kl, dn_t, preferred_element_type=jnp.float32) * scale
            s_loc = s_loc + bias_ref[0, r - rs]
            s_ctx = lax.dot_general(qi, kc, dn_t, preferred_element_type=jnp.float32) * scale
            m = jnp.maximum(jnp.max(s_loc, axis=-1, keepdims=True), jnp.max(s_ctx, axis=-1, keepdims=True))
            p_loc = jnp.exp(s_loc - m)
            p_ctx = jnp.exp(s_ctx - m)
            den = jnp.sum(p_loc, axis=-1, keepdims=True) + jnp.sum(p_ctx, axis=-1, keepdims=True)
            o = (jnp.dot(p_loc.astype(vl.dtype), vl, preferred_element_type=jnp.float32)
                 + jnp.dot(p_ctx.astype(vc.dtype), vc, preferred_element_type=jnp.float32)) / den
            o_ref[0, i * GRID_W:(i + 1) * GRID_W, :] = o.astype(o_ref.dtype)


def na_attention(z, bias, *, n_heads, head_dim, n_ctx, q_col, k_col, v_col, out_dtype=BF):
    B, ltot, _ = z.shape
    L = ltot - n_ctx
    rows = L // GRID_W
    tq = NA_ROWS_PER_STEP * GRID_W
    assert n_ctx % tq == 0 and L % tq == 0 and rows >= NA_WIN_ROWS
    body = functools.partial(_na_body, n_ctx=n_ctx, rows=rows, scale=head_dim ** -0.5)
    return pl.pallas_call(
        body,
        grid=(B, n_heads, ltot // tq),
        in_specs=[
            pl.BlockSpec((1, tq, head_dim), lambda b, h, t: (b, t, q_col + h)),
            pl.BlockSpec((1, ltot, head_dim), lambda b, h, t: (b, 0, k_col + h)),
            pl.BlockSpec((1, ltot, head_dim), lambda b, h, t: (b, 0, v_col + h)),
            pl.BlockSpec((1, NA_WIN_ROWS, GRID_W, NA_WIN_ROWS * GRID_W), lambda b, h, t: (h, 0, 0, 0)),
        ],
        out_specs=pl.BlockSpec((1, tq, head_dim), lambda b, h, t: (b, t, h)),
        out_shape=jax.ShapeDtypeStruct((B, ltot, n_heads * head_dim), out_dtype),
        compiler_params=_cparams(3),
        name="na_attention",
    )(z, z, z, bias)


def rope_tables(n_ctx, L, head_dim):
    n_freq = head_dim // 4
    freq = ROPE_BASE ** (-jnp.arange(n_freq, dtype=jnp.float32) / n_freq)
    t = jnp.arange(L)
    ang_r = (t // GRID_W).astype(jnp.float32)[:, None] * freq[None, :]
    ang_c = (t % GRID_W).astype(jnp.float32)[:, None] * freq[None, :]
    cos = jnp.concatenate([jnp.cos(ang_r), jnp.cos(ang_r), jnp.cos(ang_c), jnp.cos(ang_c)], axis=-1)
    sin = jnp.concatenate([-jnp.sin(ang_r), jnp.sin(ang_r), -jnp.sin(ang_c), jnp.sin(ang_c)], axis=-1)
    cos = jnp.concatenate([jnp.ones((n_ctx, head_dim), jnp.float32), cos], axis=0)
    sin = jnp.concatenate([jnp.zeros((n_ctx, head_dim), jnp.float32), sin], axis=0)
    return cos, sin


def _rope(x, cos, sin):
    d = x.shape[-1]
    quarter = d // 4
    lane = lax.broadcasted_iota(jnp.int32, x.shape, 1)
    first = (lane % (2 * quarter)) < quarter
    partner = jnp.where(first, pltpu.roll(x, d - quarter, 1), pltpu.roll(x, quarter, 1))
    return x * cos + partner * sin


def _ret_body(lg_ref, q_ref, k_ref, v_ref, cos_ref, sin_ref, *rest, reverse, k_scale):
    if reverse:
        of_ref, gate_ref, y_ref, s_ref = rest
    else:
        o_ref, s_ref = rest
    h = pl.program_id(1)
    c = pl.program_id(2)
    C = q_ref.shape[1]

    @pl.when(c == 0)
    def _init():
        s_ref[...] = jnp.zeros_like(s_ref)

    lg = lg_ref[h]
    cos = cos_ref[...]
    sin = sin_ref[...]
    q = _rope(q_ref[0].astype(jnp.float32), cos, sin)
    k = _rope(k_ref[0].astype(jnp.float32), cos, sin) * k_scale
    v = v_ref[0]
    pos = lax.broadcasted_iota(jnp.int32, (C, 1), 0).astype(jnp.float32)
    row = lax.broadcasted_iota(jnp.int32, (C, C), 0).astype(jnp.float32)
    colm = lax.broadcasted_iota(jnp.int32, (C, C), 1).astype(jnp.float32)
    if reverse:
        pos = (C - 1.0) - pos
        rel = colm - row
    else:
        rel = row - colm
    intra = jnp.where(rel >= 0, jnp.exp(lg * jnp.maximum(rel, 0.0)), 0.0)
    q_dec = jnp.exp(lg * (pos + 1.0))
    k_dec = jnp.exp(lg * ((C - 1.0) - pos))
    c_dec = jnp.exp(lg * C)
    scores = lax.dot_general(q.astype(BF), k.astype(BF), (((1,), (1,)), ((), ())),
                             preferred_element_type=jnp.float32) * intra
    s_old = s_ref[...]
    o = (jnp.dot(scores.astype(BF), v, preferred_element_type=jnp.float32)
         + jnp.dot((q * q_dec).astype(BF), s_old.astype(BF), preferred_element_type=jnp.float32))
    s_ref[...] = c_dec * s_old + lax.dot_general((k * k_dec).astype(BF), v, (((0,), (0,)), ((), ())),
                                                 preferred_element_type=jnp.float32)
    if reverse:
        o = o + of_ref[0]
        mu = jnp.mean(o, axis=-1, keepdims=True)
        oc = o - mu
        var = jnp.mean(oc * oc, axis=-1, keepdims=True)
        g = gate_ref[0].astype(jnp.float32)
        y_ref[0] = (oc * lax.rsqrt(var + LN_EPS) * (g * jax.nn.sigmoid(g))).astype(y_ref.dtype)
    else:
        o_ref[0] = o


def retention(z, cos, sin, log_dec_f, log_dec_b, *, n_heads, qk_dim, v_dim, n_ctx,
              q_col, k_col, v_col, g_col, out_dtype=BF):
    B, ltot, _ = z.shape
    C = RET_TILE
    assert n_ctx % C == 0 and ltot % C == 0
    n_c = ltot // C
    n_cc = n_ctx // C
    k_scale = qk_dim ** -0.5

    def fwd_idx(c):
        return c

    def bwd_idx(c):
        return jnp.where(c < n_cc, n_cc - 1 - c, n_c - 1 - (c - n_cc))

    def specs(cidx):
        return [
            pl.BlockSpec((1, C, qk_dim), lambda b, h, c, lg: (b, cidx(c), q_col + h)),
            pl.BlockSpec((1, C, qk_dim), lambda b, h, c, lg: (b, cidx(c), k_col + h)),
            pl.BlockSpec((1, C, v_dim), lambda b, h, c, lg: (b, cidx(c), v_col + h)),
            pl.BlockSpec((C, qk_dim), lambda b, h, c, lg: (cidx(c), 0)),
            pl.BlockSpec((C, qk_dim), lambda b, h, c, lg: (cidx(c), 0)),
        ]

    o_f = pl.pallas_call(
        functools.partial(_ret_body, reverse=False, k_scale=k_scale),
        grid_spec=pltpu.PrefetchScalarGridSpec(
            num_scalar_prefetch=1, grid=(B, n_heads, n_c),
            in_specs=specs(fwd_idx),
            out_specs=pl.BlockSpec((1, C, v_dim), lambda b, h, c, lg: (b, c, h)),
            scratch_shapes=[pltpu.VMEM((qk_dim, v_dim), jnp.float32)]),
        out_shape=jax.ShapeDtypeStruct((B, ltot, n_heads * v_dim), jnp.float32),
        compiler_params=_cparams(3), name="retention_fwd",
    )(log_dec_f.astype(jnp.float32), z, z, z, cos, sin)

    return pl.pallas_call(
        functools.partial(_ret_body, reverse=True, k_scale=k_scale),
        grid_spec=pltpu.PrefetchScalarGridSpec(
            num_scalar_prefetch=1, grid=(B, n_heads, n_c),
            in_specs=specs(bwd_idx) + [
                pl.BlockSpec((1, C, v_dim), lambda b, h, c, lg: (b, bwd_idx(c), h)),
                pl.BlockSpec((1, C, v_dim), lambda b, h, c, lg: (b, bwd_idx(c), g_col + h)),
            ],
            out_specs=pl.BlockSpec((1, C, v_dim), lambda b, h, c, lg: (b, bwd_idx(c), h)),
            scratch_shapes=[pltpu.VMEM((qk_dim, v_dim), jnp.float32)]),
        out_shape=jax.ShapeDtypeStruct((B, ltot, n_heads * v_dim), out_dtype),
        compiler_params=_cparams(3), name="retention_bwd",
    )(log_dec_b.astype(jnp.float32), z, z, z, cos, sin, o_f, z)


RET_CHUNK = 128


def _head_norm(x):
    mu = x.mean(-1, keepdims=True)
    var = jnp.square(x - mu).mean(-1, keepdims=True)
    return (x - mu) * lax.rsqrt(var + LN_EPS)


def _axial_rope(x, pos_row, pos_col):
    half = x.shape[-1] // 2
    n_freq = half // 2
    freq = ROPE_BASE ** (-jnp.arange(n_freq, dtype=jnp.float32) / n_freq)

    def rotate(xp, pos):
        ang = pos.astype(jnp.float32)[:, None] * freq[None, :]
        cos = jnp.cos(ang)[None, :, None, :]
        sin = jnp.sin(ang)[None, :, None, :]
        x1, x2 = xp[..., :n_freq], xp[..., n_freq:]
        return jnp.concatenate([x1 * cos - x2 * sin, x1 * sin + x2 * cos], axis=-1)

    return jnp.concatenate([rotate(x[..., :half], pos_row), rotate(x[..., half:], pos_col)], axis=-1)


def _retention_chunkwise(q, k, v, log_gamma, s0, with_output):
    B, L, H, _ = q.shape
    dv = v.shape[-1]
    n = L // RET_CHUNK
    idx = jnp.arange(RET_CHUNK, dtype=jnp.float32)
    rel = idx[:, None] - idx[None, :]
    intra = jnp.where(rel >= 0, jnp.exp(log_gamma[:, None, None] * jnp.maximum(rel, 0.0)), 0.0)
    q_dec = jnp.exp(log_gamma[None, :] * (idx[:, None] + 1.0))[None, :, :, None]
    k_dec = jnp.exp(log_gamma[None, :] * (RET_CHUNK - 1.0 - idx[:, None]))[None, :, :, None]
    c_dec = jnp.exp(log_gamma * RET_CHUNK)[None, :, None, None]

    def blocks(z):
        return z.reshape(B, n, RET_CHUNK, H, z.shape[-1]).swapaxes(0, 1)

    def step(s, qkv):
        qi, ki, vi = qkv
        s_new = c_dec * s + jnp.einsum('bchd,bche->bhde', ki * k_dec, vi)
        if not with_output:
            return s_new, None
        scores = jnp.einsum('bqhd,bkhd->bhqk', qi, ki) * intra[None]
        o = (jnp.einsum('bhqk,bkhe->bqhe', scores, vi)
             + jnp.einsum('bqhd,bhde->bqhe', qi * q_dec, s))
        return s_new, o

    s_fin, o = lax.scan(step, s0, (blocks(q), blocks(k), blocks(v)))
    if with_output:
        o = o.swapaxes(0, 1).reshape(B, L, H, dv)
    return o, s_fin


def _bidirectional_retention(q, k, v, qc, kc, vc, log_dec_f, log_dec_b):
    s0 = jnp.zeros((q.shape[0], RET_HEADS, RET_QK_DIM, RET_V_DIM), jnp.float32)
    lf = log_dec_f.astype(jnp.float32)
    lb = log_dec_b.astype(jnp.float32)

    def flip(t):
        return t[:, ::-1]

    oc_f, s_cf = _retention_chunkwise(qc, kc, vc, lf, s0, True)
    o_f, _ = _retention_chunkwise(q, k, v, lf, s_cf, True)
    oc_b, s_cb = _retention_chunkwise(flip(qc), flip(kc), flip(vc), lb, s0, True)
    o_b, _ = _retention_chunkwise(flip(q), flip(k), flip(v), lb, s_cb, True)
    return o_f + flip(o_b), oc_f + flip(oc_b)


def retention_jax(z, log_dec_f, log_dec_b, n_ctx):
    B, ltot, _ = z.shape
    L = ltot - n_ctx
    zf = z.astype(jnp.float32)
    base = 3 * NA_WIDTH
    sp = [base, base + RET_QK_WIDTH, base + 2 * RET_QK_WIDTH, base + 2 * RET_QK_WIDTH + RET_V_WIDTH,
          base + 2 * RET_QK_WIDTH + 2 * RET_V_WIDTH]
    zl = [zf[:, n_ctx:, sp[i]:sp[i + 1]] for i in range(4)]
    zc = [zf[:, :n_ctx, sp[i]:sp[i + 1]] for i in range(4)]

    def heads(t):
        return t.reshape(B, t.shape[1], RET_HEADS, -1)

    t = jnp.arange(L)
    pos_row = t // GRID_W
    pos_col = t % GRID_W
    k_scale = RET_QK_DIM ** -0.5
    qb = _axial_rope(heads(zl[0]), pos_row, pos_col)
    kb = _axial_rope(heads(zl[1]), pos_row, pos_col) * k_scale
    o, oc = _bidirectional_retention(qb, kb, heads(zl[2]), heads(zc[0]), heads(zc[1]) * k_scale, heads(zc[2]),
                                     log_dec_f, log_dec_b)

    def merge(o_b, gate):
        return _head_norm(o_b).reshape(B, o_b.shape[1], RET_V_WIDTH) * jax.nn.silu(gate)

    return jnp.concatenate([merge(oc, zc[3]), merge(o, zl[3])], axis=1).astype(BF)


def _s5_discretize(lam_re, lam_im, log_dt, b_re, b_im):
    dt = jnp.exp(log_dt)[:, None]
    logmag = lam_re * dt
    angle = lam_im * dt
    nr = jnp.exp(logmag) * jnp.cos(angle) - 1.0
    ni = jnp.exp(logmag) * jnp.sin(angle)
    den = lam_re * lam_re + lam_im * lam_im
    fr = ((nr * lam_re + ni * lam_im) / den)[..., None]
    fi = ((ni * lam_re - nr * lam_im) / den)[..., None]
    return logmag, angle, fr * b_re - fi * b_im, fr * b_im + fi * b_re


def s5_operators(lam_re, lam_im, log_dt, b_re, b_im, c_re, c_im):
    T, CH, P = S5_T, S5_GROUP_CH, S5_STATE
    G = lam_re.shape[1]
    tau = jnp.arange(T + 1, dtype=jnp.float32)[:, None, None]
    t_i = jnp.arange(T)
    kt = 0.0
    wbs, wcs, a_rows = [], [], []
    for d in range(2):
        logmag, angle, bb_re, bb_im = _s5_discretize(lam_re[d], lam_im[d], log_dt[d], b_re, b_im)
        mag = jnp.exp(logmag[None] * tau)
        pr = mag * jnp.cos(angle[None] * tau)
        pi = mag * jnp.sin(angle[None] * tau)
        cr, ci = c_re[d], c_im[d]
        z_re = pr[..., None] * bb_re[None] - pi[..., None] * bb_im[None]
        z_im = pr[..., None] * bb_im[None] + pi[..., None] * bb_re[None]
        m = (jnp.einsum('ghp,tgpk->tghk', cr, z_re, precision=HI)
             - jnp.einsum('ghp,tgpk->tghk', ci, z_im, precision=HI))
        lag = (t_i[:, None] - t_i[None, :]) if d == 0 else (t_i[None, :] - t_i[:, None])
        blk = jnp.where((lag >= 0)[:, :, None, None, None], m[jnp.clip(lag, 0, T)], 0.0)
        kt = kt + blk.transpose(2, 1, 4, 0, 3).reshape(G, T * CH, T * CH)
        dist = (T - 1 - t_i) if d == 0 else t_i
        wbs += [z_re[dist].transpose(1, 0, 3, 2).reshape(G, T * CH, P),
                z_im[dist].transpose(1, 0, 3, 2).reshape(G, T * CH, P)]
        n_t = (t_i + 1) if d == 0 else (T - t_i)
        qr = pr[n_t][:, :, None, :]
        qi = pi[n_t][:, :, None, :]
        wcs += [(cr[None] * qr - ci[None] * qi).transpose(1, 3, 0, 2).reshape(G, P, T * CH),
                (-(cr[None] * qi + ci[None] * qr)).transpose(1, 3, 0, 2).reshape(G, P, T * CH)]
        a_rows += [pr[T].reshape(-1), pi[T].reshape(-1)]

    def pair_diag(w):
        g2 = w.reshape(G // 2, 2, w.shape[1], w.shape[2])
        z = jnp.zeros_like(g2[:, 0])
        return jnp.concatenate([jnp.concatenate([g2[:, 0], z], axis=2),
                                jnp.concatenate([z, g2[:, 1]], axis=2)], axis=1)

    wb = jnp.concatenate([pair_diag(w) for w in wbs], axis=2)
    wc = jnp.concatenate([pair_diag(w) for w in wcs], axis=1)
    return kt.astype(BF), wb.astype(BF), wc.astype(BF), jnp.stack(a_rows)


def _s5_body(u_ref, kt_ref, wb_ref, wc_ref, a_ref, d_ref, y_ref, e_ref, x_ref, ug_ref, yg_ref, us_ref, *, n_c, n_cc):
    T, CH, P2, W, GB = S5_T, S5_GROUP_CH, 2 * S5_STATE, S5_T * S5_GROUP_CH, S5_TILE_GROUPS
    n_pair = GB // 2
    half = LANES // CH
    lane = lax.broadcasted_iota(jnp.int32, (n_c, LANES), 1)

    for s in range(T):
        us_ref[s] = u_ref[0, pl.ds(s, n_c, stride=T), :]

    def u_step(s):
        return us_ref[s]

    def place(x, src_slot, dst_slot):
        shift = (CH * (dst_slot - src_slot)) % LANES
        r = pltpu.roll(x, shift, 1) if shift else x
        return jnp.where((lane >= CH * dst_slot) & (lane < CH * (dst_slot + 1)), r, 0.0)

    for g in range(GB):
        for k in range(T // half):
            acc = place(u_step(half * k), g, 0)
            for s8 in range(1, half):
                acc = acc + place(u_step(half * k + s8), g, s8)
            ug_ref[:, W * g + LANES * k:W * g + LANES * (k + 1)] = acc.astype(BF)
    for q in range(n_pair):
        e = jnp.dot(ug_ref[:, 2 * W * q:2 * W * (q + 1)], wb_ref[q], preferred_element_type=jnp.float32)
        for comp in range(4):
            e_ref[comp, :, q * P2:(q + 1) * P2] = e[:, comp * P2:(comp + 1) * P2]
    a_fr, a_fi, a_br, a_bi = a_ref[0:1, :], a_ref[1:2, :], a_ref[2:3, :], a_ref[3:4, :]

    def step(i, carry):
        xfr, xfi, xbr, xbi = carry
        j = jnp.where(i < n_cc, n_cc - 1 - i, n_c - 1 - (i - n_cc))
        x_ref[0, pl.ds(i, 1), :] = xfr
        x_ref[1, pl.ds(i, 1), :] = xfi
        x_ref[2, pl.ds(j, 1), :] = xbr
        x_ref[3, pl.ds(j, 1), :] = xbi
        return (a_fr * xfr - a_fi * xfi + e_ref[0, pl.ds(i, 1), :],
                a_fr * xfi + a_fi * xfr + e_ref[1, pl.ds(i, 1), :],
                a_br * xbr - a_bi * xbi + e_ref[2, pl.ds(j, 1), :],
                a_br * xbi + a_bi * xbr + e_ref[3, pl.ds(j, 1), :])

    zero = jnp.zeros((1, n_pair * P2), jnp.float32)
    lax.fori_loop(0, n_c, step, (zero, zero, zero, zero))
    for q in range(n_pair):
        xcat = jnp.concatenate([x_ref[comp, :, q * P2:(q + 1) * P2] for comp in range(4)], axis=1).astype(BF)
        yx = jnp.dot(xcat, wc_ref[q], preferred_element_type=jnp.float32)
        for gi in range(2):
            g = 2 * q + gi
            yg_ref[:, W * g:W * (g + 1)] = (
                jnp.dot(ug_ref[:, W * g:W * (g + 1)], kt_ref[g], preferred_element_type=jnp.float32)
                + yx[:, gi * W:(gi + 1) * W])
    d = d_ref[...]
    for s in range(T):
        k, s8 = divmod(s, half)
        acc = place(yg_ref[:, LANES * k:LANES * (k + 1)], s8, 0)
        for g in range(1, GB):
            acc = acc + place(yg_ref[:, W * g + LANES * k:W * g + LANES * (k + 1)], s8, g)
        y_ref[0, pl.ds(s, n_c, stride=T), :] = jax.nn.gelu(acc + d * u_step(s))


def s5_scan_gelu(u, ops, d_skip, *, n_ctx):
    kt, wb, wc, a = ops
    B, ltot, D = u.shape
    T, CH, P, GB = S5_T, S5_GROUP_CH, S5_STATE, S5_TILE_GROUPS
    W = T * CH
    n_c = ltot // T
    body = functools.partial(_s5_body, n_c=n_c, n_cc=n_ctx // T)
    tile = pl.BlockSpec((1, ltot, LANES), lambda j, b: (b, 0, j))
    return pl.pallas_call(
        body,
        grid=(D // LANES, B),
        in_specs=[
            tile,
            pl.BlockSpec((GB, W, W), lambda j, b: (j, 0, 0)),
            pl.BlockSpec((GB // 2, 2 * W, 8 * P), lambda j, b: (j, 0, 0)),
            pl.BlockSpec((GB // 2, 8 * P, 2 * W), lambda j, b: (j, 0, 0)),
            pl.BlockSpec((4, GB * P), lambda j, b: (0, j)),
            pl.BlockSpec((1, LANES), lambda j, b: (0, j)),
        ],
        out_specs=tile,
        out_shape=jax.ShapeDtypeStruct((B, ltot, D), jnp.float32),
        scratch_shapes=[pltpu.VMEM((4, n_c, GB * P), jnp.float32),
                        pltpu.VMEM((4, n_c, GB * P), jnp.float32),
                        pltpu.VMEM((n_c, GB * W), BF),
                        pltpu.VMEM((n_c, GB * W), jnp.float32),
                        pltpu.VMEM((T, n_c, LANES), jnp.float32)],
        compiler_params=_cparams(2, S5_VMEM_LIMIT_BYTES),
        name="s5_scan",
    )(u, kt, wb, wc, a, d_skip.reshape(1, D))


ROUTER_LANES = 128


def _router_body(h_ref, w_ref, b_ref, o_ref):
    logits = jnp.dot(h_ref[...], w_ref[...], preferred_element_type=jnp.float32) + b_ref[...]
    lane = lax.broadcasted_iota(jnp.int32, logits.shape, 1)

    def first_argmax(vals, vmax):
        return jnp.min(jnp.where(vals == vmax, lane, ROUTER_LANES), axis=-1, keepdims=True)

    gl = jnp.where(lane < MOE_GROUPS, logits, NEG_BIG)
    gmax = jnp.max(gl, axis=-1, keepdims=True)
    gsum = jnp.sum(jnp.where(lane < MOE_GROUPS, jnp.exp(gl - gmax), 0.0), axis=-1, keepdims=True)
    g_w = 1.0 / gsum
    lo = MOE_GROUPS + MOE_EXPERTS_PER_GROUP * first_argmax(gl, gmax)
    el = jnp.where((lane >= lo) & (lane < lo + MOE_EXPERTS_PER_GROUP), logits, NEG_BIG)
    m1 = jnp.max(el, axis=-1, keepdims=True)
    i1 = first_argmax(el, m1)
    el2 = jnp.where(lane == i1, NEG_BIG, el)
    m2 = jnp.max(el2, axis=-1, keepdims=True)
    i2 = first_argmax(el2, m2)
    e2 = jnp.exp(m2 - m1)
    den = 1.0 + e2
    o_ref[...] = jnp.where(lane == 0, (i1 - MOE_GROUPS).astype(jnp.float32),
                           jnp.where(lane == 1, (i2 - MOE_GROUPS).astype(jnp.float32),
                                     jnp.where(lane == 2, g_w * (1.0 / den),
                                               jnp.where(lane == 3, g_w * (e2 / den), 0.0))))


def moe_router(h, w_group, b_group, w_expert, b_expert):
    n, d = h.shape
    n_log = MOE_GROUPS + MOE_EXPERTS
    w = jnp.zeros((d, ROUTER_LANES), jnp.float32).at[:, :MOE_GROUPS].set(w_group).at[:, MOE_GROUPS:n_log].set(w_expert)
    b = jnp.zeros((1, ROUTER_LANES), jnp.float32).at[0, :MOE_GROUPS].set(b_group).at[0, MOE_GROUPS:n_log].set(b_expert)
    tm = MM_TILE_M if n % MM_TILE_M == 0 else n
    route = pl.pallas_call(
        _router_body,
        grid=(n // tm,),
        in_specs=[pl.BlockSpec((tm, d), lambda i: (i, 0)),
                  pl.BlockSpec((d, ROUTER_LANES), lambda i: (0, 0)),
                  pl.BlockSpec((1, ROUTER_LANES), lambda i: (0, 0))],
        out_specs=pl.BlockSpec((tm, ROUTER_LANES), lambda i: (i, 0)),
        out_shape=jax.ShapeDtypeStruct((n, ROUTER_LANES), jnp.float32),
        compiler_params=_cparams(1),
        name="moe_router",
    )(h, w.astype(BF), b)
    return route[:, 0:2].astype(jnp.int32), route[:, 2:4]


def _expert_body(blk_e_ref, blk_on_ref, x_ref, wg_ref, wu_ref, wd_ref, y_ref):
    i = pl.program_id(0)

    @pl.when(blk_on_ref[i] != 0)
    def _compute():
        x = x_ref[...]
        g = jnp.dot(x, wg_ref[0], preferred_element_type=jnp.float32)
        u = jnp.dot(x, wu_ref[0], preferred_element_type=jnp.float32)
        hid = (g * jax.nn.sigmoid(g) * u).astype(BF)
        y_ref[...] = jnp.dot(hid, wd_ref[0], preferred_element_type=jnp.float32).astype(y_ref.dtype)

    @pl.when(blk_on_ref[i] == 0)
    def _unused_block():
        y_ref[...] = jnp.zeros_like(y_ref)


def expert_mlp(xb, blk_e, blk_on, w_gate, w_up, w_down, out_dtype=jnp.float32):
    n_rows, d = xb.shape
    hid = w_gate.shape[2]
    return pl.pallas_call(
        _expert_body,
        grid_spec=pltpu.PrefetchScalarGridSpec(
            num_scalar_prefetch=2, grid=(n_rows // EXPERT_BLOCK,),
            in_specs=[pl.BlockSpec((EXPERT_BLOCK, d), lambda i, be, on: (i, 0)),
                      pl.BlockSpec((1, d, hid), lambda i, be, on: (be[i], 0, 0)),
                      pl.BlockSpec((1, d, hid), lambda i, be, on: (be[i], 0, 0)),
                      pl.BlockSpec((1, hid, d), lambda i, be, on: (be[i], 0, 0))],
            out_specs=pl.BlockSpec((EXPERT_BLOCK, d), lambda i, be, on: (i, 0))),
        out_shape=jax.ShapeDtypeStruct((n_rows, d), out_dtype),
        compiler_params=_cparams(1),
        name="expert_mlp",
    )(blk_e, blk_on, xb, w_gate, w_up, w_down)


def moe_dispatch(eid, n_tok):
    n_asg = n_tok * MOE_TOP_K
    flat_e = eid.reshape(-1)
    order = jnp.argsort(flat_e)
    e_sorted = flat_e[order]
    counts = jnp.sum(flat_e[:, None] == jnp.arange(MOE_EXPERTS)[None, :], axis=0).astype(jnp.int32)
    padded = (counts + EXPERT_BLOCK - 1) // EXPERT_BLOCK * EXPERT_BLOCK
    pad_end = jnp.cumsum(padded)
    pad_start = pad_end - padded
    start = jnp.cumsum(counts) - counts
    dest_sorted = pad_start[e_sorted] + jnp.arange(n_asg, dtype=jnp.int32) - start[e_sorted]
    n_blocks = (n_asg + MOE_EXPERTS * (EXPERT_BLOCK - 1)) // EXPERT_BLOCK + 1
    n_rows = n_blocks * EXPERT_BLOCK
    blk_first = jnp.arange(n_blocks, dtype=jnp.int32) * EXPERT_BLOCK
    blk_e = jnp.minimum(jnp.sum(blk_first[:, None] >= pad_end[None, :], axis=1), MOE_EXPERTS - 1).astype(jnp.int32)
    blk_on = (blk_first < pad_end[-1]).astype(jnp.int32)
    row = jnp.arange(n_rows, dtype=jnp.int32)
    row_e = jnp.repeat(blk_e, EXPERT_BLOCK)
    pos = row - pad_start[row_e]
    valid = (pos < counts[row_e]) & (row < pad_end[-1])
    src = jnp.clip(start[row_e] + pos, 0, n_asg - 1)
    row_tok = jnp.where(valid, order[src] // MOE_TOP_K, n_tok).astype(jnp.int32)
    dest = dest_sorted[jnp.argsort(order)].reshape(n_tok, MOE_TOP_K)
    return row_tok, dest, blk_e, blk_on


def hier_moe(h, w_group, b_group, w_expert, b_expert, w_gate, w_up, w_down):
    n_tok, d = h.shape
    eid, wts = moe_router(h, w_group, b_group, w_expert, b_expert)
    row_tok, dest, blk_e, blk_on = moe_dispatch(eid, n_tok)
    h_pad = jnp.concatenate([h, jnp.zeros((1, d), h.dtype)], axis=0)
    yb = expert_mlp(h_pad[row_tok], blk_e, blk_on, w_gate.astype(BF), w_up.astype(BF), w_down.astype(BF), BF)
    return (yb[dest[:, 0]].astype(jnp.float32) * wts[:, 0:1]
            + yb[dest[:, 1]].astype(jnp.float32) * wts[:, 1:2])


def kernel(x, c, ctx, c_ctx, mod_w, mod_b, ln_g, ln_b, mix_w_in, mix_w_out, na_rpb,
           ret_log_decay_fwd, ret_log_decay_bwd, s5_w_in, s5_lam_re, s5_lam_im, s5_log_dt,
           s5_b_re, s5_b_im, s5_c_re, s5_c_im, s5_d, s5_w_glu, s5_w_out,
           moe_w_group, moe_b_group, moe_w_expert, moe_b_expert, moe_w_gate, moe_w_up, moe_w_down):
    B, L, D = x.shape
    n_ctx = ctx.shape[1]
    ltot = n_ctx + L
    n_tok = B * ltot
    xa = jnp.concatenate([ctx, x], axis=1)
    cond = jnp.concatenate([jax.nn.silu(c), jax.nn.silu(c_ctx)[None]], axis=0)
    mods = []
    for layer in range(DEPTH):
        m = (cond @ mod_w[layer] + mod_b[layer]).reshape(B + 1, 6, 1, D)
        mods.append([m[:, i] for i in range(6)])
    cos, sin = rope_tables(n_ctx, L, RET_QK_DIM)

    h = modulate(xa, mods[0][0], mods[0][1], n_ctx=n_ctx)
    for layer in range(DEPTH):
        last = layer == DEPTH - 1
        m = mods[layer]
        j = layer // 2
        h2d = h.reshape(n_tok, D)
        if layer % 2 == 0:
            z = _mm(h2d, mix_w_in[j].astype(BF), BF).reshape(B, ltot, -1)
            o_na = na_attention(z, na_bias_table(na_rpb[j]), n_heads=NA_HEADS, head_dim=NA_HEAD_DIM,
                                n_ctx=n_ctx, q_col=0, k_col=NA_HEADS, v_col=2 * NA_HEADS)
            y_ret = retention_jax(z, ret_log_decay_fwd[j], ret_log_decay_bwd[j], n_ctx)
            y = jnp.concatenate([o_na, y_ret], axis=-1).reshape(n_tok, D)
            o = _mm(y, mix_w_out[j].astype(BF))
        else:
            u = _mm(h2d, s5_w_in[j].astype(BF)).reshape(B, ltot, D)
            ops = s5_operators(s5_lam_re[j], s5_lam_im[j], s5_log_dt[j], s5_b_re[j], s5_b_im[j],
                               s5_c_re[j], s5_c_im[j])
            g = s5_scan_gelu(u, ops, s5_d[j], n_ctx=n_ctx).astype(BF).reshape(n_tok, D)
            gg = _mm(g, s5_w_glu[j].astype(BF), BF, glu_gate=g)
            o = _mm(gg, s5_w_out[j].astype(BF))
        x1, h_moe = residual_ln(xa, o.reshape(B, ltot, D), m[2], ln_g[layer, 0], ln_b[layer, 0],
                                n_ctx=n_ctx, next_shift=m[3], next_scale=m[4])
        f = hier_moe(h_moe.reshape(n_tok, D),
                     moe_w_group[layer], moe_b_group[layer], moe_w_expert[layer], moe_b_expert[layer],
                     moe_w_gate[layer], moe_w_up[layer], moe_w_down[layer]).reshape(B, ltot, D)
        if last:
            return residual_ln(x1, f, m[5], ln_g[layer, 1], ln_b[layer, 1], n_ctx=n_ctx, latent_only=True)
        xa, h = residual_ln(x1, f, m[5], ln_g[layer, 1], ln_b[layer, 1], n_ctx=n_ctx,
                            next_shift=mods[layer + 1][0], next_scale=mods[layer + 1][1])
```

```python
import functools

import jax
import jax.numpy as jnp
import numpy as np
from jax import lax
from jax.experimental import pallas as pl
from jax.experimental.pallas import tpu as pltpu

D_MODEL = 4096
DEPTH = 2
GRID_W = 64
NA_HEAD_DIM = 128
NA_HEADS = D_MODEL // 2 // NA_HEAD_DIM
NA_WIDTH = NA_HEADS * NA_HEAD_DIM
NA_WIN_ROWS = 8
NA_WIN_COLS = 16
RET_HEADS = 8
RET_V_DIM = D_MODEL // 2 // RET_HEADS
RET_QK_DIM = RET_V_DIM // 2
RET_QK_WIDTH = RET_HEADS * RET_QK_DIM
RET_V_WIDTH = RET_HEADS * RET_V_DIM
S5_GROUP_CH = 16
S5_STATE = 64
S5_T = 16
LANES = 128
S5_TILE_GROUPS = LANES // S5_GROUP_CH
MOE_GROUPS = 4
MOE_EXPERTS_PER_GROUP = 8
MOE_EXPERTS = MOE_GROUPS * MOE_EXPERTS_PER_GROUP
MOE_TOP_K = 2
EXPERT_BLOCK = 256
ROPE_BASE = 10000.0
LN_EPS = 1e-5
DEEPNORM_ALPHA = (2.0 * DEPTH) ** 0.25
NEG_BIG = -1e30

VMEM_LIMIT_BYTES = 48 * 1024 * 1024
S5_VMEM_LIMIT_BYTES = 56 * 1024 * 1024
MM_TILE_M = 512
MM_TILE_N = 512
ROW_TILE = 256
RET_TILE = 256
NA_ROWS_PER_STEP = 4
HI = lax.Precision.HIGHEST
BF = jnp.bfloat16


def _cparams(n_axes, vmem_limit_bytes=VMEM_LIMIT_BYTES):
    return pltpu.CompilerParams(dimension_semantics=("arbitrary",) * n_axes,
                                vmem_limit_bytes=vmem_limit_bytes)


def _mm_body(x_ref, w_ref, o_ref):
    o_ref[...] = jnp.dot(x_ref[...], w_ref[...],
                         preferred_element_type=jnp.float32).astype(o_ref.dtype)


def _mm_glu_body(x_ref, w_ref, g_ref, o_ref):
    t = jnp.dot(x_ref[...], w_ref[...], preferred_element_type=jnp.float32)
    o_ref[...] = (g_ref[...].astype(jnp.float32) * jax.nn.sigmoid(t)).astype(o_ref.dtype)


def _mm(x, w, out_dtype=jnp.float32, glu_gate=None):
    m, k = x.shape
    n = w.shape[1]
    tm = MM_TILE_M if m % MM_TILE_M == 0 else m
    tn = MM_TILE_N if n % MM_TILE_N == 0 else n
    in_specs = [pl.BlockSpec((tm, k), lambda i, j: (i, 0)),
                pl.BlockSpec((k, tn), lambda i, j: (0, j))]
    args = [x, w]
    body = _mm_body
    if glu_gate is not None:
        in_specs.append(pl.BlockSpec((tm, tn), lambda i, j: (i, j)))
        args.append(glu_gate)
        body = _mm_glu_body
    return pl.pallas_call(
        body,
        grid=(m // tm, n // tn),
        in_specs=in_specs,
        out_specs=pl.BlockSpec((tm, tn), lambda i, j: (i, j)),
        out_shape=jax.ShapeDtypeStruct((m, n), out_dtype),
        compiler_params=_cparams(2),
        name="dense_mm",
    )(*args)


def _mod_row_spec(n_ctx_blocks, n_batch):
    return pl.BlockSpec((1, 1, D_MODEL), lambda b, t: (jnp.where(t < n_ctx_blocks, n_batch, b), 0, 0))


def _modulate_body(x_ref, shift_ref, scale_ref, h_ref):
    h_ref[0] = (x_ref[0] * (1.0 + scale_ref[0]) + shift_ref[0]).astype(h_ref.dtype)


def modulate(x, shift, scale, *, n_ctx):
    B, n, d = x.shape
    blk = pl.BlockSpec((1, ROW_TILE, d), lambda b, t: (b, t, 0))
    mod = _mod_row_spec(n_ctx // ROW_TILE, B)
    return pl.pallas_call(
        _modulate_body,
        grid=(B, n // ROW_TILE),
        in_specs=[blk, mod, mod],
        out_specs=blk,
        out_shape=jax.ShapeDtypeStruct((B, n, d), BF),
        compiler_params=_cparams(2),
        name="modulate",
    )(x, shift, scale)


def _ln_body(x_ref, o_ref, gate_ref, g_ref, b_ref, *rest, with_next):
    if with_next:
        shift_ref, scale_ref, x1_ref, h_ref = rest
    else:
        (x1_ref,) = rest
    y = DEEPNORM_ALPHA * x_ref[0] + gate_ref[0] * o_ref[0].astype(jnp.float32)
    mu = jnp.mean(y, axis=-1, keepdims=True)
    yc = y - mu
    var = jnp.mean(yc * yc, axis=-1, keepdims=True)
    x1 = yc * lax.rsqrt(var + LN_EPS) * g_ref[...] + b_ref[...]
    x1_ref[0] = x1
    if with_next:
        h_ref[0] = (x1 * (1.0 + scale_ref[0]) + shift_ref[0]).astype(h_ref.dtype)


def residual_ln(x, o, gate, ln_g, ln_b, *, n_ctx, next_shift=None, next_scale=None, latent_only=False):
    B, n, d = x.shape
    n_cb = n_ctx // ROW_TILE
    blk = pl.BlockSpec((1, ROW_TILE, d), lambda b, t: (b, t, 0))
    mod = _mod_row_spec(n_cb, B)
    vec = pl.BlockSpec((1, d), lambda b, t: (0, 0))
    with_next = next_shift is not None
    in_specs = [blk, blk, mod, vec, vec]
    args = [x, o, gate, ln_g.reshape(1, d), ln_b.reshape(1, d)]
    if latent_only:
        x1_spec = pl.BlockSpec((1, ROW_TILE, d), lambda b, t: (b, jnp.maximum(t - n_cb, 0), 0))
        x1_shape = jax.ShapeDtypeStruct((B, n - n_ctx, d), jnp.float32)
    else:
        x1_spec = blk
        x1_shape = jax.ShapeDtypeStruct((B, n, d), jnp.float32)
    out_specs, out_shape = x1_spec, x1_shape
    if with_next:
        in_specs += [mod, mod]
        args += [next_shift, next_scale]
        out_specs = [x1_spec, blk]
        out_shape = [x1_shape, jax.ShapeDtypeStruct((B, n, d), BF)]
    return pl.pallas_call(
        functools.partial(_ln_body, with_next=with_next),
        grid=(B, n // ROW_TILE),
        in_specs=in_specs,
        out_specs=out_specs,
        out_shape=out_shape,
        compiler_params=_cparams(2),
        name="residual_ln",
    )(*args)


def na_bias_table(rpb):
    H = rpb.shape[0]
    col = np.arange(GRID_W)
    col_start = np.clip(col - NA_WIN_COLS // 2, 0, GRID_W - NA_WIN_COLS)
    col_ok = (col[None, :] >= col_start[:, None]) & (col[None, :] < col_start[:, None] + NA_WIN_COLS)
    col_idx = np.clip(col[None, :] - col[:, None] + NA_WIN_COLS - 1, 0, 2 * NA_WIN_COLS - 2)
    offs = np.arange(NA_WIN_ROWS)
    row_idx = offs[None, :] - offs[:, None] + (NA_WIN_ROWS - 1)
    col_sel = (np.arange(2 * NA_WIN_COLS - 1)[:, None, None] == col_idx[None]).astype(np.float32)
    row_sel = (np.arange(2 * NA_WIN_ROWS - 1)[:, None, None] == row_idx[None]).astype(np.float32)
    b = jnp.einsum('hab,aoj,bqk->hoqjk', rpb.astype(jnp.float32), row_sel, col_sel, precision=HI)
    b = jnp.where(col_ok[None, None, :, None, :], b, NEG_BIG)
    return b.reshape(H, NA_WIN_ROWS, GRID_W, NA_WIN_ROWS * GRID_W)


def _na_body(q_ref, k_ref, v_ref, bias_ref, o_ref, *, n_ctx, rows, scale):
    t = pl.program_id(2)
    n_cb = n_ctx // (NA_ROWS_PER_STEP * GRID_W)
    kc = k_ref[0, 0:n_ctx, :]
    vc = v_ref[0, 0:n_ctx, :]
    dn_t = (((1,), (1,)), ((), ()))

    @pl.when(t < n_cb)
    def _ctx():
        s = lax.dot_general(q_ref[0], kc, dn_t, preferred_element_type=jnp.float32) * scale
        m = jnp.max(s, axis=-1, keepdims=True)
        p = jnp.exp(s - m)
        den = jnp.sum(p, axis=-1, keepdims=True)
        o = jnp.dot(p.astype(vc.dtype), vc, preferred_element_type=jnp.float32) / den
        o_ref[0] = o.astype(o_ref.dtype)

    @pl.when(t >= n_cb)
    def _lat():
        r0 = (t - n_cb) * NA_ROWS_PER_STEP
        n_loc = NA_WIN_ROWS * GRID_W
        for i in range(NA_ROWS_PER_STEP):
            r = r0 + i
            rs = jnp.clip(r - NA_WIN_ROWS // 2, 0, rows - NA_WIN_ROWS)
            qi = q_ref[0, i * GRID_W:(i + 1) * GRID_W, :]
            start = pl.multiple_of(n_ctx + rs * GRID_W, GRID_W)
            kl = k_ref[0, pl.ds(start, n_loc), :]
            vl = v_ref[0, pl.ds(start, n_loc), :]
            s_loc = lax.dot_general(qi, kl, dn_t, preferred_element_type=jnp.float32) * scale
            s_loc = s_loc + bias_ref[0, r - rs]
            s_ctx = lax.dot_general(qi, kc, dn_t, preferred_element_type=jnp.float32) * scale
            m = jnp.maximum(jnp.max(s_loc, axis=-1, keepdims=True), jnp.max(s_ctx, axis=-1, keepdims=True))
            p_loc = jnp.exp(s_loc - m)
            p_ctx = jnp.exp(s_ctx - m)
            den = jnp.sum(p_loc, axis=-1, keepdims=True) + jnp.sum(p_ctx, axis=-1, keepdims=True)
            o = (jnp.dot(p_loc.astype(vl.dtype), vl, preferred_element_type=jnp.float32)
                 + jnp.dot(p_ctx.astype(vc.dtype), vc, preferred_element_type=jnp.float32)) / den
            o_ref[0, i * GRID_W:(i + 1) * GRID_W, :] = o.astype(o_ref.dtype)


def na_attention(z, bias, *, n_heads, head_dim, n_ctx, q_col, k_col, v_col, out_dtype=BF):
    B, ltot, _ = z.shape
    L = ltot - n_ctx
    rows = L // GRID_W
    tq = NA_ROWS_PER_STEP * GRID_W
    assert n_ctx % tq == 0 and L % tq == 0 and rows >= NA_WIN_ROWS
    body = functools.partial(_na_body, n_ctx=n_ctx, rows=rows, scale=head_dim ** -0.5)
    return pl.pallas_call(
        body,
        grid=(B, n_heads, ltot // tq),
        in_specs=[
            pl.BlockSpec((1, tq, head_dim), lambda b, h, t: (b, t, q_col + h)),
            pl.BlockSpec((1, ltot, head_dim), lambda b, h, t: (b, 0, k_col + h)),
            pl.BlockSpec((1, ltot, head_dim), lambda b, h, t: (b, 0, v_col + h)),
            pl.BlockSpec((1, NA_WIN_ROWS, GRID_W, NA_WIN_ROWS * GRID_W), lambda b, h, t: (h, 0, 0, 0)),
        ],
        out_specs=pl.BlockSpec((1, tq, head_dim), lambda b, h, t: (b, t, h)),
        out_shape=jax.ShapeDtypeStruct((B, ltot, n_heads * head_dim), out_dtype),
        compiler_params=_cparams(3),
        name="na_attention",
    )(z, z, z, bias)


def rope_tables(n_ctx, L, head_dim):
    n_freq = head_dim // 4
    freq = ROPE_BASE ** (-jnp.arange(n_freq, dtype=jnp.float32) / n_freq)
    t = jnp.arange(L)
    ang_r = (t // GRID_W).astype(jnp.float32)[:, None] * freq[None, :]
    ang_c = (t % GRID_W).astype(jnp.float32)[:, None] * freq[None, :]
    cos = jnp.concatenate([jnp.cos(ang_r), jnp.cos(ang_r), jnp.cos(ang_c), jnp.cos(ang_c)], axis=-1)
    sin = jnp.concatenate([-jnp.sin(ang_r), jnp.sin(ang_r), -jnp.sin(ang_c), jnp.sin(ang_c)], axis=-1)
    cos = jnp.concatenate([jnp.ones((n_ctx, head_dim), jnp.float32), cos], axis=0)
    sin = jnp.concatenate([jnp.zeros((n_ctx, head_dim), jnp.float32), sin], axis=0)
    return cos, sin


def _rope(x, cos, sin):
    d = x.shape[-1]
    quarter = d // 4
    lane = lax.broadcasted_iota(jnp.int32, x.shape, 1)
    first = (lane % (2 * quarter)) < quarter
    partner = jnp.where(first, pltpu.roll(x, d - quarter, 1), pltpu.roll(x, quarter, 1))
    return x * cos + partner * sin


def _ret_body(lg_ref, q_ref, k_ref, v_ref, cos_ref, sin_ref, *rest, reverse, k_scale):
    if reverse:
        of_ref, gate_ref, y_ref, s_ref = rest
    else:
        o_ref, s_ref = rest
    h = pl.program_id(1)
    c = pl.program_id(2)
    C = q_ref.shape[1]

    @pl.when(c == 0)
    def _init():
        s_ref[...] = jnp.zeros_like(s_ref)

    lg = lg_ref[h]
    cos = cos_ref[...]
    sin = sin_ref[...]
    q = _rope(q_ref[0].astype(jnp.float32), cos, sin)
    k = _rope(k_ref[0].astype(jnp.float32), cos, sin) * k_scale
    v = v_ref[0]
    pos = lax.broadcasted_iota(jnp.int32, (C, 1), 0).astype(jnp.float32)
    row = lax.broadcasted_iota(jnp.int32, (C, C), 0).astype(jnp.float32)
    colm = lax.broadcasted_iota(jnp.int32, (C, C), 1).astype(jnp.float32)
    if reverse:
        pos = (C - 1.0) - pos
        rel = colm - row
    else:
        rel = row - colm
    intra = jnp.where(rel >= 0, jnp.exp(lg * jnp.maximum(rel, 0.0)), 0.0)
    q_dec = jnp.exp(lg * (pos + 1.0))
    k_dec = jnp.exp(lg * ((C - 1.0) - pos))
    c_dec = jnp.exp(lg * C)
    scores = lax.dot_general(q.astype(BF), k.astype(BF), (((1,), (1,)), ((), ())),
                             preferred_element_type=jnp.float32) * intra
    s_old = s_ref[...]
    o = (jnp.dot(scores.astype(BF), v, preferred_element_type=jnp.float32)
         + jnp.dot((q * q_dec).astype(BF), s_old.astype(BF), preferred_element_type=jnp.float32))
    s_ref[...] = c_dec * s_old + lax.dot_general((k * k_dec).astype(BF), v, (((0,), (0,)), ((), ())),
                                                 preferred_element_type=jnp.float32)
    if reverse:
        o = o + of_ref[0]
        mu = jnp.mean(o, axis=-1, keepdims=True)
        oc = o - mu
        var = jnp.mean(oc * oc, axis=-1, keepdims=True)
        g = gate_ref[0].astype(jnp.float32)
        y_ref[0] = (oc * lax.rsqrt(var + LN_EPS) * (g * jax.nn.sigmoid(g))).astype(y_ref.dtype)
    else:
        o_ref[0] = o


def retention(z, cos, sin, log_dec_f, log_dec_b, *, n_heads, qk_dim, v_dim, n_ctx,
              q_col, k_col, v_col, g_col, out_dtype=BF):
    B, ltot, _ = z.shape
    C = RET_TILE
    assert n_ctx % C == 0 and ltot % C == 0
    n_c = ltot // C
    n_cc = n_ctx // C
    k_scale = qk_dim ** -0.5

    def fwd_idx(c):
        return c

    def bwd_idx(c):
        return jnp.where(c < n_cc, n_cc - 1 - c, n_c - 1 - (c - n_cc))

    def specs(cidx):
        return [
            pl.BlockSpec((1, C, qk_dim), lambda b, h, c, lg: (b, cidx(c), q_col + h)),
            pl.BlockSpec((1, C, qk_dim), lambda b, h, c, lg: (b, cidx(c), k_col + h)),
            pl.BlockSpec((1, C, v_dim), lambda b, h, c, lg: (b, cidx(c), v_col + h)),
            pl.BlockSpec((C, qk_dim), lambda b, h, c, lg: (cidx(c), 0)),
            pl.BlockSpec((C, qk_dim), lambda b, h, c, lg: (cidx(c), 0)),
        ]

    o_f = pl.pallas_call(
        functools.partial(_ret_body, reverse=False, k_scale=k_scale),
        grid_spec=pltpu.PrefetchScalarGridSpec(
            num_scalar_prefetch=1, grid=(B, n_heads, n_c),
            in_specs=specs(fwd_idx),
            out_specs=pl.BlockSpec((1, C, v_dim), lambda b, h, c, lg: (b, c, h)),
            scratch_shapes=[pltpu.VMEM((qk_dim, v_dim), jnp.float32)]),
        out_shape=jax.ShapeDtypeStruct((B, ltot, n_heads * v_dim), jnp.float32),
        compiler_params=_cparams(3), name="retention_fwd",
    )(log_dec_f.astype(jnp.float32), z, z, z, cos, sin)

    return pl.pallas_call(
        functools.partial(_ret_body, reverse=True, k_scale=k_scale),
        grid_spec=pltpu.PrefetchScalarGridSpec(
            num_scalar_prefetch=1, grid=(B, n_heads, n_c),
            in_specs=specs(bwd_idx) + [
                pl.BlockSpec((1, C, v_dim), lambda b, h, c, lg: (b, bwd_idx(c), h)),
                pl.BlockSpec((1, C, v_dim), lambda b, h, c, lg: (b, bwd_idx(c), g_col + h)),
            ],
            out_specs=pl.BlockSpec((1, C, v_dim), lambda b, h, c, lg: (b, bwd_idx(c), h)),
            scratch_shapes=[pltpu.VMEM((qk_dim, v_dim), jnp.float32)]),
        out_shape=jax.ShapeDtypeStruct((B, ltot, n_heads * v_dim), out_dtype),
        compiler_params=_cparams(3), name="retention_bwd",
    )(log_dec_b.astype(jnp.float32), z, z, z, cos, sin, o_f, z)


RET_CHUNK = 128


def _head_norm(x):
    mu = x.mean(-1, keepdims=True)
    var = jnp.square(x - mu).mean(-1, keepdims=True)
    return (x - mu) * lax.rsqrt(var + LN_EPS)


def _axial_rope(x, pos_row, pos_col):
    half = x.shape[-1] // 2
    n_freq = half // 2
    freq = ROPE_BASE ** (-jnp.arange(n_freq, dtype=jnp.float32) / n_freq)

    def rotate(xp, pos):
        ang = pos.astype(jnp.float32)[:, None] * freq[None, :]
        cos = jnp.cos(ang)[None, :, None, :]
        sin = jnp.sin(ang)[None, :, None, :]
        x1, x2 = xp[..., :n_freq], xp[..., n_freq:]
        return jnp.concatenate([x1 * cos - x2 * sin, x1 * sin + x2 * cos], axis=-1)

    return jnp.concatenate([rotate(x[..., :half], pos_row), rotate(x[..., half:], pos_col)], axis=-1)


def _retention_chunkwise(q, k, v, log_gamma, s0, with_output):
    B, L, H, _ = q.shape
    dv = v.shape[-1]
    n = L // RET_CHUNK
    idx = jnp.arange(RET_CHUNK, dtype=jnp.float32)
    rel = idx[:, None] - idx[None, :]
    intra = jnp.where(rel >= 0, jnp.exp(log_gamma[:, None, None] * jnp.maximum(rel, 0.0)), 0.0)
    q_dec = jnp.exp(log_gamma[None, :] * (idx[:, None] + 1.0))[None, :, :, None]
    k_dec = jnp.exp(log_gamma[None, :] * (RET_CHUNK - 1.0 - idx[:, None]))[None, :, :, None]
    c_dec = jnp.exp(log_gamma * RET_CHUNK)[None, :, None, None]

    def blocks(z):
        return z.reshape(B, n, RET_CHUNK, H, z.shape[-1]).swapaxes(0, 1)

    def step(s, qkv):
        qi, ki, vi = qkv
        s_new = c_dec * s + jnp.einsum('bchd,bche->bhde', ki * k_dec, vi)
        if not with_output:
            return s_new, None
        scores = jnp.einsum('bqhd,bkhd->bhqk', qi, ki) * intra[None]
        o = (jnp.einsum('bhqk,bkhe->bqhe', scores, vi)
             + jnp.einsum('bqhd,bhde->bqhe', qi * q_dec, s))
        return s_new, o

    s_fin, o = lax.scan(step, s0, (blocks(q), blocks(k), blocks(v)))
    if with_output:
        o = o.swapaxes(0, 1).reshape(B, L, H, dv)
    return o, s_fin


def _bidirectional_retention(q, k, v, qc, kc, vc, log_dec_f, log_dec_b):
    s0 = jnp.zeros((q.shape[0], RET_HEADS, RET_QK_DIM, RET_V_DIM), jnp.float32)
    lf = log_dec_f.astype(jnp.float32)
    lb = log_dec_b.astype(jnp.float32)

    def flip(t):
        return t[:, ::-1]

    oc_f, s_cf = _retention_chunkwise(qc, kc, vc, lf, s0, True)
    o_f, _ = _retention_chunkwise(q, k, v, lf, s_cf, True)
    oc_b, s_cb = _retention_chunkwise(flip(qc), flip(kc), flip(vc), lb, s0, True)
    o_b, _ = _retention_chunkwise(flip(q), flip(k), flip(v), lb, s_cb, True)
    return o_f + flip(o_b), oc_f + flip(oc_b)


def retention_jax(z, log_dec_f, log_dec_b, n_ctx):
    B, ltot, _ = z.shape
    L = ltot - n_ctx
    zf = z.astype(jnp.float32)
    base = 3 * NA_WIDTH
    sp = [base, base + RET_QK_WIDTH, base + 2 * RET_QK_WIDTH, base + 2 * RET_QK_WIDTH + RET_V_WIDTH,
          base + 2 * RET_QK_WIDTH + 2 * RET_V_WIDTH]
    zl = [zf[:, n_ctx:, sp[i]:sp[i + 1]] for i in range(4)]
    zc = [zf[:, :n_ctx, sp[i]:sp[i + 1]] for i in range(4)]

    def heads(t):
        return t.reshape(B, t.shape[1], RET_HEADS, -1)

    t = jnp.arange(L)
    pos_row = t // GRID_W
    pos_col = t % GRID_W
    k_scale = RET_QK_DIM ** -0.5
    qb = _axial_rope(heads(zl[0]), pos_row, pos_col)
    kb = _axial_rope(heads(zl[1]), pos_row, pos_col) * k_scale
    o, oc = _bidirectional_retention(qb, kb, heads(zl[2]), heads(zc[0]), heads(zc[1]) * k_scale, heads(zc[2]),
                                     log_dec_f, log_dec_b)

    def merge(o_b, gate):
        return _head_norm(o_b).reshape(B, o_b.shape[1], RET_V_WIDTH) * jax.nn.silu(gate)

    return jnp.concatenate([merge(oc, zc[3]), merge(o, zl[3])], axis=1).astype(BF)


def _s5_discretize(lam_re, lam_im, log_dt, b_re, b_im):
    dt = jnp.exp(log_dt)[:, None]
    logmag = lam_re * dt
    angle = lam_im * dt
    nr = jnp.exp(logmag) * jnp.cos(angle) - 1.0
    ni = jnp.exp(logmag) * jnp.sin(angle)
    den = lam_re * lam_re + lam_im * lam_im
    fr = ((nr * lam_re + ni * lam_im) / den)[..., None]
    fi = ((ni * lam_re - nr * lam_im) / den)[..., None]
    return logmag, angle, fr * b_re - fi * b_im, fr * b_im + fi * b_re


def s5_operators(lam_re, lam_im, log_dt, b_re, b_im, c_re, c_im):
    T, CH, P = S5_T, S5_GROUP_CH, S5_STATE
    W = T * CH
    G = lam_re.shape[1]
    tau = jnp.arange(T + 1, dtype=jnp.float32)[:, None, None]
    zeros_w = jnp.zeros((G, CH, W), jnp.float32)
    kt_t = 0.0
    wbs, wcs, a_rows = [], [], []
    for d in range(2):
        logmag, angle, bb_re, bb_im = _s5_discretize(lam_re[d], lam_im[d], log_dt[d], b_re, b_im)
        mag = jnp.exp(logmag[None] * tau)
        pr = mag * jnp.cos(angle[None] * tau)
        pi = mag * jnp.sin(angle[None] * tau)
        cr, ci = c_re[d], c_im[d]
        bt_re = bb_re.transpose(0, 2, 1)
        bt_im = bb_im.transpose(0, 2, 1)
        z_re = pr[:, :, None, :] * bt_re[None] - pi[:, :, None, :] * bt_im[None]
        z_im = pr[:, :, None, :] * bt_im[None] + pi[:, :, None, :] * bt_re[None]
        zl_re = z_re[:T][::-1] if d == 0 else z_re[:T]
        zl_im = z_im[:T][::-1] if d == 0 else z_im[:T]
        m = (jnp.einsum('ghp,xgkp->ghxk', cr, zl_re, precision=HI)
             - jnp.einsum('ghp,xgkp->ghxk', ci, zl_im, precision=HI)).reshape(G, CH, W)
        padded = jnp.concatenate([m, zeros_w] if d == 0 else [zeros_w, m], axis=-1)
        rows = [padded[:, :, CH * (T - 1 - t):CH * (T - 1 - t) + W] if d == 0
                else padded[:, :, W - CH * t:2 * W - CH * t] for t in range(T)]
        kt_t = kt_t + jnp.stack(rows, axis=1).reshape(G, W, W)
        wbs += [zl_re.transpose(1, 0, 2, 3).reshape(G, W, P),
                zl_im.transpose(1, 0, 2, 3).reshape(G, W, P)]
        qr = (pr[1:] if d == 0 else pr[1:][::-1])[:, :, None, :]
        qi = (pi[1:] if d == 0 else pi[1:][::-1])[:, :, None, :]
        wcs += [(cr[None] * qr - ci[None] * qi).transpose(1, 0, 2, 3).reshape(G, W, P),
                (-(cr[None] * qi + ci[None] * qr)).transpose(1, 0, 2, 3).reshape(G, W, P)]
        a_rows += [pr[T].reshape(-1), pi[T].reshape(-1)]

    def pair_diag(w):
        g2 = w.reshape(G // 2, 2, w.shape[1], w.shape[2])
        z = jnp.zeros_like(g2[:, 0])
        return jnp.concatenate([jnp.concatenate([g2[:, 0], z], axis=2),
                                jnp.concatenate([z, g2[:, 1]], axis=2)], axis=1)

    wb = jnp.concatenate([pair_diag(w) for w in wbs], axis=2)
    wc_t = jnp.concatenate([pair_diag(w) for w in wcs], axis=2)
    return kt_t.astype(BF), wb.astype(BF), wc_t.astype(BF), jnp.stack(a_rows)


def _s5_body(u_ref, kt_ref, wb_ref, wc_ref, a_ref, d_ref, y_ref, e_ref, x_ref, ug_ref, yg_ref, us_ref, *, n_c, n_cc):
    T, CH, P2, W, GB = S5_T, S5_GROUP_CH, 2 * S5_STATE, S5_T * S5_GROUP_CH, S5_TILE_GROUPS
    n_pair = GB // 2
    half = LANES // CH
    lane = lax.broadcasted_iota(jnp.int32, (n_c, LANES), 1)

    for s in range(T):
        us_ref[s] = u_ref[0, pl.ds(s, n_c, stride=T), :]

    def u_step(s):
        return us_ref[s]

    def place(x, src_slot, dst_slot):
        shift = (CH * (dst_slot - src_slot)) % LANES
        r = pltpu.roll(x, shift, 1) if shift else x
        return jnp.where((lane >= CH * dst_slot) & (lane < CH * (dst_slot + 1)), r, 0.0)

    for g in range(GB):
        for k in range(T // half):
            acc = place(u_step(half * k), g, 0)
            for s8 in range(1, half):
                acc = acc + place(u_step(half * k + s8), g, s8)
            ug_ref[:, W * g + LANES * k:W * g + LANES * (k + 1)] = acc.astype(BF)
    for q in range(n_pair):
        e = jnp.dot(ug_ref[:, 2 * W * q:2 * W * (q + 1)], wb_ref[q], preferred_element_type=jnp.float32)
        for comp in range(4):
            e_ref[comp, :, q * P2:(q + 1) * P2] = e[:, comp * P2:(comp + 1) * P2]
    a_fr, a_fi, a_br, a_bi = a_ref[0:1, :], a_ref[1:2, :], a_ref[2:3, :], a_ref[3:4, :]

    def step(i, carry):
        xfr, xfi, xbr, xbi = carry
        j = jnp.where(i < n_cc, n_cc - 1 - i, n_c - 1 - (i - n_cc))
        x_ref[0, pl.ds(i, 1), :] = xfr
        x_ref[1, pl.ds(i, 1), :] = xfi
        x_ref[2, pl.ds(j, 1), :] = xbr
        x_ref[3, pl.ds(j, 1), :] = xbi
        return (a_fr * xfr - a_fi * xfi + e_ref[0, pl.ds(i, 1), :],
                a_fr * xfi + a_fi * xfr + e_ref[1, pl.ds(i, 1), :],
                a_br * xbr - a_bi * xbi + e_ref[2, pl.ds(j, 1), :],
                a_br * xbi + a_bi * xbr + e_ref[3, pl.ds(j, 1), :])

    zero = jnp.zeros((1, n_pair * P2), jnp.float32)
    lax.fori_loop(0, n_c, step, (zero, zero, zero, zero))
    nt_dims = (((1,), (1,)), ((), ()))
    for q in range(n_pair):
        xcat = jnp.concatenate([x_ref[comp, :, q * P2:(q + 1) * P2] for comp in range(4)], axis=1).astype(BF)
        yx = lax.dot_general(xcat, wc_ref[q], nt_dims, preferred_element_type=jnp.float32)
        for gi in range(2):
            g = 2 * q + gi
            yg_ref[:, W * g:W * (g + 1)] = (
                lax.dot_general(ug_ref[:, W * g:W * (g + 1)], kt_ref[g], nt_dims, preferred_element_type=jnp.float32)
                + yx[:, gi * W:(gi + 1) * W])
    d = d_ref[...]
    for s in range(T):
        k, s8 = divmod(s, half)
        acc = place(yg_ref[:, LANES * k:LANES * (k + 1)], s8, 0)
        for g in range(1, GB):
            acc = acc + place(yg_ref[:, W * g + LANES * k:W * g + LANES * (k + 1)], s8, g)
        y_ref[0, pl.ds(s, n_c, stride=T), :] = jax.nn.gelu(acc + d * u_step(s))


def s5_scan_gelu(u, ops, d_skip, *, n_ctx):
    kt, wb, wc, a = ops
    B, ltot, D = u.shape
    T, CH, P, GB = S5_T, S5_GROUP_CH, S5_STATE, S5_TILE_GROUPS
    W = T * CH
    n_c = ltot // T
    body = functools.partial(_s5_body, n_c=n_c, n_cc=n_ctx // T)
    tile = pl.BlockSpec((1, ltot, LANES), lambda j, b: (b, 0, j))
    return pl.pallas_call(
        body,
        grid=(D // LANES, B),
        in_specs=[
            tile,
            pl.BlockSpec((GB, W, W), lambda j, b: (j, 0, 0)),
            pl.BlockSpec((GB // 2, 2 * W, 8 * P), lambda j, b: (j, 0, 0)),
            pl.BlockSpec((GB // 2, 8 * P, 2 * W), lambda j, b: (j, 0, 0)),
            pl.BlockSpec((4, GB * P), lambda j, b: (0, j)),
            pl.BlockSpec((1, LANES), lambda j, b: (0, j)),
        ],
        out_specs=tile,
        out_shape=jax.ShapeDtypeStruct((B, ltot, D), jnp.float32),
        scratch_shapes=[pltpu.VMEM((4, n_c, GB * P), jnp.float32),
                        pltpu.VMEM((4, n_c, GB * P), jnp.float32),
                        pltpu.VMEM((n_c, GB * W), BF),
                        pltpu.VMEM((n_c, GB * W), jnp.float32),
                        pltpu.VMEM((T, n_c, LANES), jnp.float32)],
        compiler_params=_cparams(2, S5_VMEM_LIMIT_BYTES),
        name="s5_scan",
    )(u, kt, wb, wc, a, d_skip.reshape(1, D))


ROUTER_LANES = 128


def _router_body(h_ref, w_ref, b_ref, o_ref):
    logits = jnp.dot(h_ref[...], w_ref[...], preferred_element_type=jnp.float32) + b_ref[...]
    lane = lax.broadcasted_iota(jnp.int32, logits.shape, 1)

    def first_argmax(vals, vmax):
        return jnp.min(jnp.where(vals == vmax, lane, ROUTER_LANES), axis=-1, keepdims=True)

    gl = jnp.where(lane < MOE_GROUPS, logits, NEG_BIG)
    gmax = jnp.max(gl, axis=-1, keepdims=True)
    gsum = jnp.sum(jnp.where(lane < MOE_GROUPS, jnp.exp(gl - gmax), 0.0), axis=-1, keepdims=True)
    g_w = 1.0 / gsum
    lo = MOE_GROUPS + MOE_EXPERTS_PER_GROUP * first_argmax(gl, gmax)
    el = jnp.where((lane >= lo) & (lane < lo + MOE_EXPERTS_PER_GROUP), logits, NEG_BIG)
    m1 = jnp.max(el, axis=-1, keepdims=True)
    i1 = first_argmax(el, m1)
    el2 = jnp.where(lane == i1, NEG_BIG, el)
    m2 = jnp.max(el2, axis=-1, keepdims=True)
    i2 = first_argmax(el2, m2)
    e2 = jnp.exp(m2 - m1)
    den = 1.0 + e2
    o_ref[...] = jnp.where(lane == 0, (i1 - MOE_GROUPS).astype(jnp.float32),
                           jnp.where(lane == 1, (i2 - MOE_GROUPS).astype(jnp.float32),
                                     jnp.where(lane == 2, g_w * (1.0 / den),
                                               jnp.where(lane == 3, g_w * (e2 / den), 0.0))))


def moe_router(h, w_group, b_group, w_expert, b_expert):
    n, d = h.shape
    n_log = MOE_GROUPS + MOE_EXPERTS
    w = jnp.zeros((d, ROUTER_LANES), jnp.float32).at[:, :MOE_GROUPS].set(w_group).at[:, MOE_GROUPS:n_log].set(w_expert)
    b = jnp.zeros((1, ROUTER_LANES), jnp.float32).at[0, :MOE_GROUPS].set(b_group).at[0, MOE_GROUPS:n_log].set(b_expert)
    tm = MM_TILE_M if n % MM_TILE_M == 0 else n
    route = pl.pallas_call(
        _router_body,
        grid=(n // tm,),
        in_specs=[pl.BlockSpec((tm, d), lambda i: (i, 0)),
                  pl.BlockSpec((d, ROUTER_LANES), lambda i: (0, 0)),
                  pl.BlockSpec((1, ROUTER_LANES), lambda i: (0, 0))],
        out_specs=pl.BlockSpec((tm, ROUTER_LANES), lambda i: (i, 0)),
        out_shape=jax.ShapeDtypeStruct((n, ROUTER_LANES), jnp.float32),
        compiler_params=_cparams(1),
        name="moe_router",
    )(h, w.astype(BF), b)
    return route[:, 0:2].astype(jnp.int32), route[:, 2:4]


def _expert_body(blk_e_ref, blk_on_ref, x_ref, wg_ref, wu_ref, wd_ref, y_ref):
    i = pl.program_id(0)

    @pl.when(blk_on_ref[i] != 0)
    def _compute():
        x = x_ref[...]
        g = jnp.dot(x, wg_ref[0], preferred_element_type=jnp.float32)
        u = jnp.dot(x, wu_ref[0], preferred_element_type=jnp.float32)
        hid = (g * jax.nn.sigmoid(g) * u).astype(BF)
        y_ref[...] = jnp.dot(hid, wd_ref[0], preferred_element_type=jnp.float32).astype(y_ref.dtype)

    @pl.when(blk_on_ref[i] == 0)
    def _unused_block():
        y_ref[...] = jnp.zeros_like(y_ref)


def expert_mlp(xb, blk_e, blk_on, w_gate, w_up, w_down, out_dtype=jnp.float32):
    n_rows, d = xb.shape
    hid = w_gate.shape[2]
    return pl.pallas_call(
        _expert_body,
        grid_spec=pltpu.PrefetchScalarGridSpec(
            num_scalar_prefetch=2, grid=(n_rows // EXPERT_BLOCK,),
            in_specs=[pl.BlockSpec((EXPERT_BLOCK, d), lambda i, be, on: (i, 0)),
                      pl.BlockSpec((1, d, hid), lambda i, be, on: (be[i], 0, 0)),
                      pl.BlockSpec((1, d, hid), lambda i, be, on: (be[i], 0, 0)),
                      pl.BlockSpec((1, hid, d), lambda i, be, on: (be[i], 0, 0))],
            out_specs=pl.BlockSpec((EXPERT_BLOCK, d), lambda i, be, on: (i, 0))),
        out_shape=jax.ShapeDtypeStruct((n_rows, d), out_dtype),
        compiler_params=_cparams(1),
        name="expert_mlp",
    )(blk_e, blk_on, xb, w_gate, w_up, w_down)


def moe_dispatch(eid, n_tok):
    n_asg = n_tok * MOE_TOP_K
    flat_e = eid.reshape(-1)
    order = jnp.argsort(flat_e)
    e_sorted = flat_e[order]
    counts = jnp.sum(flat_e[:, None] == jnp.arange(MOE_EXPERTS)[None, :], axis=0).astype(jnp.int32)
    padded = (counts + EXPERT_BLOCK - 1) // EXPERT_BLOCK * EXPERT_BLOCK
    pad_end = jnp.cumsum(padded)
    pad_start = pad_end - padded
    start = jnp.cumsum(counts) - counts
    dest_sorted = pad_start[e_sorted] + jnp.arange(n_asg, dtype=jnp.int32) - start[e_sorted]
    n_blocks = (n_asg + MOE_EXPERTS * (EXPERT_BLOCK - 1)) // EXPERT_BLOCK + 1
    n_rows = n_blocks * EXPERT_BLOCK
    blk_first = jnp.arange(n_blocks, dtype=jnp.int32) * EXPERT_BLOCK
    blk_e = jnp.minimum(jnp.sum(blk_first[:, None] >= pad_end[None, :], axis=1), MOE_EXPERTS - 1).astype(jnp.int32)
    blk_on = (blk_first < pad_end[-1]).astype(jnp.int32)
    row = jnp.arange(n_rows, dtype=jnp.int32)
    row_e = jnp.repeat(blk_e, EXPERT_BLOCK)
    pos = row - pad_start[row_e]
    valid = (pos < counts[row_e]) & (row < pad_end[-1])
    src = jnp.clip(start[row_e] + pos, 0, n_asg - 1)
    row_tok = jnp.where(valid, order[src] // MOE_TOP_K, n_tok).astype(jnp.int32)
    dest = dest_sorted[jnp.argsort(order)].reshape(n_tok, MOE_TOP_K)
    return row_tok, dest, blk_e, blk_on


def hier_moe(h, w_group, b_group, w_expert, b_expert, w_gate, w_up, w_down):
    n_tok, d = h.shape
    eid, wts = moe_router(h, w_group, b_group, w_expert, b_expert)
    row_tok, dest, blk_e, blk_on = moe_dispatch(eid, n_tok)
    h_pad = jnp.concatenate([h, jnp.zeros((1, d), h.dtype)], axis=0)
    yb = expert_mlp(h_pad[row_tok], blk_e, blk_on, w_gate.astype(BF), w_up.astype(BF), w_down.astype(BF), BF)
    return (yb[dest[:, 0]].astype(jnp.float32) * wts[:, 0:1]
            + yb[dest[:, 1]].astype(jnp.float32) * wts[:, 1:2])


def kernel(x, c, ctx, c_ctx, mod_w, mod_b, ln_g, ln_b, mix_w_in, mix_w_out, na_rpb,
           ret_log_decay_fwd, ret_log_decay_bwd, s5_w_in, s5_lam_re, s5_lam_im, s5_log_dt,
           s5_b_re, s5_b_im, s5_c_re, s5_c_im, s5_d, s5_w_glu, s5_w_out,
           moe_w_group, moe_b_group, moe_w_expert, moe_b_expert, moe_w_gate, moe_w_up, moe_w_down):
    B, L, D = x.shape
    n_ctx = ctx.shape[1]
    ltot = n_ctx + L
    n_tok = B * ltot
    xa = jnp.concatenate([ctx, x], axis=1)
    cond = jnp.concatenate([jax.nn.silu(c), jax.nn.silu(c_ctx)[None]], axis=0)
    mods = []
    for layer in range(DEPTH):
        m = (cond @ mod_w[layer] + mod_b[layer]).reshape(B + 1, 6, 1, D)
        mods.append([m[:, i] for i in range(6)])
    cos, sin = rope_tables(n_ctx, L, RET_QK_DIM)

    h = modulate(xa, mods[0][0], mods[0][1], n_ctx=n_ctx)
    for layer in range(DEPTH):
        last = layer == DEPTH - 1
        m = mods[layer]
        j = layer // 2
        h2d = h.reshape(n_tok, D)
        if layer % 2 == 0:
            z = _mm(h2d, mix_w_in[j].astype(BF), BF).reshape(B, ltot, -1)
            o_na = na_attention(z, na_bias_table(na_rpb[j]), n_heads=NA_HEADS, head_dim=NA_HEAD_DIM,
                                n_ctx=n_ctx, q_col=0, k_col=NA_HEADS, v_col=2 * NA_HEADS)
            y_ret = retention_jax(z, ret_log_decay_fwd[j], ret_log_decay_bwd[j], n_ctx)
            y = jnp.concatenate([o_na, y_ret], axis=-1).reshape(n_tok, D)
            o = _mm(y, mix_w_out[j].astype(BF))
        else:
            u = _mm(h2d, s5_w_in[j].astype(BF)).reshape(B, ltot, D)
            ops = s5_operators(s5_lam_re[j], s5_lam_im[j], s5_log_dt[j], s5_b_re[j], s5_b_im[j],
                               s5_c_re[j], s5_c_im[j])
            g = s5_scan_gelu(u, ops, s5_d[j], n_ctx=n_ctx).astype(BF).reshape(n_tok, D)
            gg = _mm(g, s5_w_glu[j].astype(BF), BF, glu_gate=g)
            o = _mm(gg, s5_w_out[j].astype(BF))
        x1, h_moe = residual_ln(xa, o.reshape(B, ltot, D), m[2], ln_g[layer, 0], ln_b[layer, 0],
                                n_ctx=n_ctx, next_shift=m[3], next_scale=m[4])
        f = hier_moe(h_moe.reshape(n_tok, D),
                     moe_w_group[layer], moe_b_group[layer], moe_w_expert[layer], moe_b_expert[layer],
                     moe_w_gate[layer], moe_w_up[layer], moe_w_down[layer]).reshape(B, ltot, D)
        if last:
            return residual_ln(x1, f, m[5], ln_g[layer, 1], ln_b[layer, 1], n_ctx=n_ctx, latent_only=True)
        xa, h = residual_ln(x1, f, m[5], ln_g[layer, 1], ln_b[layer, 1], n_ctx=n_ctx,
                            next_shift=mods[layer + 1][0], next_scale=mods[layer + 1][1])
```

```python
import functools

import jax
import jax.numpy as jnp
import numpy as np
from jax import lax
from jax.experimental import pallas as pl
from jax.experimental.pallas import tpu as pltpu

D_MODEL = 4096
DEPTH = 2
GRID_W = 64
NA_HEAD_DIM = 128
NA_HEADS = D_MODEL // 2 // NA_HEAD_DIM
NA_WIDTH = NA_HEADS * NA_HEAD_DIM
NA_WIN_ROWS = 8
NA_WIN_COLS = 16
RET_HEADS = 8
RET_V_DIM = D_MODEL // 2 // RET_HEADS
RET_QK_DIM = RET_V_DIM // 2
RET_QK_WIDTH = RET_HEADS * RET_QK_DIM
RET_V_WIDTH = RET_HEADS * RET_V_DIM
S5_GROUP_CH = 16
S5_STATE = 64
S5_T = 16
LANES = 128
S5_TILE_GROUPS = LANES // S5_GROUP_CH
MOE_GROUPS = 4
MOE_EXPERTS_PER_GROUP = 8
MOE_EXPERTS = MOE_GROUPS * MOE_EXPERTS_PER_GROUP
MOE_TOP_K = 2
EXPERT_BLOCK = 256
ROPE_BASE = 10000.0
LN_EPS = 1e-5
DEEPNORM_ALPHA = (2.0 * DEPTH) ** 0.25
NEG_BIG = -1e30

VMEM_LIMIT_BYTES = 48 * 1024 * 1024
S5_VMEM_LIMIT_BYTES = 56 * 1024 * 1024
MM_TILE_M = 512
MM_TILE_N = 1024
ROW_TILE = 256
RET_TILE = 256
NA_ROWS_PER_STEP = 4
HI = lax.Precision.HIGHEST
BF = jnp.bfloat16


def _cparams(n_axes, vmem_limit_bytes=VMEM_LIMIT_BYTES):
    return pltpu.CompilerParams(dimension_semantics=("arbitrary",) * n_axes,
                                vmem_limit_bytes=vmem_limit_bytes)


def _mm_body(x_ref, w_ref, o_ref):
    o_ref[...] = jnp.dot(x_ref[...], w_ref[...],
                         preferred_element_type=jnp.float32).astype(o_ref.dtype)


def _mm_glu_body(x_ref, w_ref, g_ref, o_ref):
    t = jnp.dot(x_ref[...], w_ref[...], preferred_element_type=jnp.float32)
    o_ref[...] = (g_ref[...].astype(jnp.float32) * jax.nn.sigmoid(t)).astype(o_ref.dtype)


def _mm(x, w, out_dtype=jnp.float32, glu_gate=None):
    m, k = x.shape
    n = w.shape[1]
    tm = MM_TILE_M if m % MM_TILE_M == 0 else m
    tn = MM_TILE_N if n % MM_TILE_N == 0 else n
    in_specs = [pl.BlockSpec((tm, k), lambda i, j: (i, 0)),
                pl.BlockSpec((k, tn), lambda i, j: (0, j))]
    args = [x, w]
    body = _mm_body
    if glu_gate is not None:
        in_specs.append(pl.BlockSpec((tm, tn), lambda i, j: (i, j)))
        args.append(glu_gate)
        body = _mm_glu_body
    return pl.pallas_call(
        body,
        grid=(m // tm, n // tn),
        in_specs=in_specs,
        out_specs=pl.BlockSpec((tm, tn), lambda i, j: (i, j)),
        out_shape=jax.ShapeDtypeStruct((m, n), out_dtype),
        compiler_params=_cparams(2),
        name="dense_mm",
    )(*args)


def _mod_row_spec(n_ctx_blocks, n_batch):
    return pl.BlockSpec((1, 1, D_MODEL), lambda b, t: (jnp.where(t < n_ctx_blocks, n_batch, b), 0, 0))


def _modulate_body(x_ref, shift_ref, scale_ref, h_ref):
    h_ref[0] = (x_ref[0] * (1.0 + scale_ref[0]) + shift_ref[0]).astype(h_ref.dtype)


def modulate(x, shift, scale, *, n_ctx):
    B, n, d = x.shape
    blk = pl.BlockSpec((1, ROW_TILE, d), lambda b, t: (b, t, 0))
    mod = _mod_row_spec(n_ctx // ROW_TILE, B)
    return pl.pallas_call(
        _modulate_body,
        grid=(B, n // ROW_TILE),
        in_specs=[blk, mod, mod],
        out_specs=blk,
        out_shape=jax.ShapeDtypeStruct((B, n, d), BF),
        compiler_params=_cparams(2),
        name="modulate",
    )(x, shift, scale)


def _ln_body(x_ref, o_ref, gate_ref, g_ref, b_ref, *rest, with_next):
    if with_next:
        shift_ref, scale_ref, x1_ref, h_ref = rest
    else:
        (x1_ref,) = rest
    y = DEEPNORM_ALPHA * x_ref[0] + gate_ref[0] * o_ref[0].astype(jnp.float32)
    mu = jnp.mean(y, axis=-1, keepdims=True)
    yc = y - mu
    var = jnp.mean(yc * yc, axis=-1, keepdims=True)
    x1 = yc * lax.rsqrt(var + LN_EPS) * g_ref[...] + b_ref[...]
    x1_ref[0] = x1
    if with_next:
        h_ref[0] = (x1 * (1.0 + scale_ref[0]) + shift_ref[0]).astype(h_ref.dtype)


def residual_ln(x, o, gate, ln_g, ln_b, *, n_ctx, next_shift=None, next_scale=None, latent_only=False):
    B, n, d = x.shape
    n_cb = n_ctx // ROW_TILE
    blk = pl.BlockSpec((1, ROW_TILE, d), lambda b, t: (b, t, 0))
    mod = _mod_row_spec(n_cb, B)
    vec = pl.BlockSpec((1, d), lambda b, t: (0, 0))
    with_next = next_shift is not None
    in_specs = [blk, blk, mod, vec, vec]
    args = [x, o, gate, ln_g.reshape(1, d), ln_b.reshape(1, d)]
    if latent_only:
        x1_spec = pl.BlockSpec((1, ROW_TILE, d), lambda b, t: (b, jnp.maximum(t - n_cb, 0), 0))
        x1_shape = jax.ShapeDtypeStruct((B, n - n_ctx, d), jnp.float32)
    else:
        x1_spec = blk
        x1_shape = jax.ShapeDtypeStruct((B, n, d), jnp.float32)
    out_specs, out_shape = x1_spec, x1_shape
    if with_next:
        in_specs += [mod, mod]
        args += [next_shift, next_scale]
        out_specs = [x1_spec, blk]
        out_shape = [x1_shape, jax.ShapeDtypeStruct((B, n, d), BF)]
    return pl.pallas_call(
        functools.partial(_ln_body, with_next=with_next),
        grid=(B, n // ROW_TILE),
        in_specs=in_specs,
        out_specs=out_specs,
        out_shape=out_shape,
        compiler_params=_cparams(2),
        name="residual_ln",
    )(*args)


def na_bias_table(rpb):
    H = rpb.shape[0]
    col = np.arange(GRID_W)
    col_start = np.clip(col - NA_WIN_COLS // 2, 0, GRID_W - NA_WIN_COLS)
    col_ok = (col[None, :] >= col_start[:, None]) & (col[None, :] < col_start[:, None] + NA_WIN_COLS)
    col_idx = np.clip(col[None, :] - col[:, None] + NA_WIN_COLS - 1, 0, 2 * NA_WIN_COLS - 2)
    offs = np.arange(NA_WIN_ROWS)
    row_idx = offs[None, :] - offs[:, None] + (NA_WIN_ROWS - 1)
    col_sel = (np.arange(2 * NA_WIN_COLS - 1)[:, None, None] == col_idx[None]).astype(np.float32)
    row_sel = (np.arange(2 * NA_WIN_ROWS - 1)[:, None, None] == row_idx[None]).astype(np.float32)
    b = jnp.einsum('hab,aoj,bqk->hoqjk', rpb.astype(jnp.float32), row_sel, col_sel, precision=HI)
    b = jnp.where(col_ok[None, None, :, None, :], b, NEG_BIG)
    return b.reshape(H, NA_WIN_ROWS, GRID_W, NA_WIN_ROWS * GRID_W)


def _na_body(q_ref, k_ref, v_ref, bias_ref, o_ref, *, n_ctx, rows, scale):
    t = pl.program_id(2)
    n_cb = n_ctx // (NA_ROWS_PER_STEP * GRID_W)
    kc = k_ref[0, 0:n_ctx, :]
    vc = v_ref[0, 0:n_ctx, :]
    dn_t = (((1,), (1,)), ((), ()))

    @pl.when(t < n_cb)
    def _ctx():
        s = lax.dot_general(q_ref[0], kc, dn_t, preferred_element_type=jnp.float32) * scale
        m = jnp.max(s, axis=-1, keepdims=True)
        p = jnp.exp(s - m)
        den = jnp.sum(p, axis=-1, keepdims=True)
        o = jnp.dot(p.astype(vc.dtype), vc, preferred_element_type=jnp.float32) / den
        o_ref[0] = o.astype(o_ref.dtype)

    @pl.when(t >= n_cb)
    def _lat():
        r0 = (t - n_cb) * NA_ROWS_PER_STEP
        n_loc = NA_WIN_ROWS * GRID_W
        for i in range(NA_ROWS_PER_STEP):
            r = r0 + i
            rs = jnp.clip(r - NA_WIN_ROWS // 2, 0, rows - NA_WIN_ROWS)
            qi = q_ref[0, i * GRID_W:(i + 1) * GRID_W, :]
            start = pl.multiple_of(n_ctx + rs * GRID_W, GRID_W)
            kl = k_ref[0, pl.ds(start, n_loc), :]
            vl = v_ref[0, pl.ds(start, n_loc), :]
            s_loc = lax.dot_general(qi, kl, dn_t, preferred_element_type=jnp.float32) * scale
            s_loc = s_loc + bias_ref[0, r - rs]
            s_ctx = lax.dot_general(qi, kc, dn_t, preferred_element_type=jnp.float32) * scale
            m = jnp.maximum(jnp.max(s_loc, axis=-1, keepdims=True), jnp.max(s_ctx, axis=-1, keepdims=True))
            p_loc = jnp.exp(s_loc - m)
            p_ctx = jnp.exp(s_ctx - m)
            den = jnp.sum(p_loc, axis=-1, keepdims=True) + jnp.sum(p_ctx, axis=-1, keepdims=True)
            o = (jnp.dot(p_loc.astype(vl.dtype), vl, preferred_element_type=jnp.float32)
                 + jnp.dot(p_ctx.astype(vc.dtype), vc, preferred_element_type=jnp.float32)) / den
            o_ref[0, i * GRID_W:(i + 1) * GRID_W, :] = o.astype(o_ref.dtype)


def na_attention(z, bias, *, n_heads, head_dim, n_ctx, q_col, k_col, v_col, out_dtype=BF):
    B, ltot, _ = z.shape
    L = ltot - n_ctx
    rows = L // GRID_W
    tq = NA_ROWS_PER_STEP * GRID_W
    assert n_ctx % tq == 0 and L % tq == 0 and rows >= NA_WIN_ROWS
    body = functools.partial(_na_body, n_ctx=n_ctx, rows=rows, scale=head_dim ** -0.5)
    return pl.pallas_call(
        body,
        grid=(B, n_heads, ltot // tq),
        in_specs=[
            pl.BlockSpec((1, tq, head_dim), lambda b, h, t: (b, t, q_col + h)),
            pl.BlockSpec((1, ltot, head_dim), lambda b, h, t: (b, 0, k_col + h)),
            pl.BlockSpec((1, ltot, head_dim), lambda b, h, t: (b, 0, v_col + h)),
            pl.BlockSpec((1, NA_WIN_ROWS, GRID_W, NA_WIN_ROWS * GRID_W), lambda b, h, t: (h, 0, 0, 0)),
        ],
        out_specs=pl.BlockSpec((1, tq, head_dim), lambda b, h, t: (b, t, h)),
        out_shape=jax.ShapeDtypeStruct((B, ltot, n_heads * head_dim), out_dtype),
        compiler_params=_cparams(3),
        name="na_attention",
    )(z, z, z, bias)


def rope_tables(n_ctx, L, head_dim):
    n_freq = head_dim // 4
    freq = ROPE_BASE ** (-jnp.arange(n_freq, dtype=jnp.float32) / n_freq)
    t = jnp.arange(L)
    ang_r = (t // GRID_W).astype(jnp.float32)[:, None] * freq[None, :]
    ang_c = (t % GRID_W).astype(jnp.float32)[:, None] * freq[None, :]
    cos = jnp.concatenate([jnp.cos(ang_r), jnp.cos(ang_r), jnp.cos(ang_c), jnp.cos(ang_c)], axis=-1)
    sin = jnp.concatenate([-jnp.sin(ang_r), jnp.sin(ang_r), -jnp.sin(ang_c), jnp.sin(ang_c)], axis=-1)
    cos = jnp.concatenate([jnp.ones((n_ctx, head_dim), jnp.float32), cos], axis=0)
    sin = jnp.concatenate([jnp.zeros((n_ctx, head_dim), jnp.float32), sin], axis=0)
    return cos, sin


def _rope(x, cos, sin):
    d = x.shape[-1]
    quarter = d // 4
    lane = lax.broadcasted_iota(jnp.int32, x.shape, 1)
    first = (lane % (2 * quarter)) < quarter
    partner = jnp.where(first, pltpu.roll(x, d - quarter, 1), pltpu.roll(x, quarter, 1))
    return x * cos + partner * sin


def _ret_body(lg_ref, q_ref, k_ref, v_ref, cos_ref, sin_ref, *rest, reverse, k_scale):
    if reverse:
        of_ref, gate_ref, y_ref, s_ref = rest
    else:
        o_ref, s_ref = rest
    h = pl.program_id(1)
    c = pl.program_id(2)
    C = q_ref.shape[1]

    @pl.when(c == 0)
    def _init():
        s_ref[...] = jnp.zeros_like(s_ref)

    lg = lg_ref[h]
    cos = cos_ref[...]
    sin = sin_ref[...]
    q = _rope(q_ref[0].astype(jnp.float32), cos, sin)
    k = _rope(k_ref[0].astype(jnp.float32), cos, sin) * k_scale
    v = v_ref[0]
    pos = lax.broadcasted_iota(jnp.int32, (C, 1), 0).astype(jnp.float32)
    row = lax.broadcasted_iota(jnp.int32, (C, C), 0).astype(jnp.float32)
    colm = lax.broadcasted_iota(jnp.int32, (C, C), 1).astype(jnp.float32)
    if reverse:
        pos = (C - 1.0) - pos
        rel = colm - row
    else:
        rel = row - colm
    intra = jnp.where(rel >= 0, jnp.exp(lg * jnp.maximum(rel, 0.0)), 0.0)
    q_dec = jnp.exp(lg * (pos + 1.0))
    k_dec = jnp.exp(lg * ((C - 1.0) - pos))
    c_dec = jnp.exp(lg * C)
    scores = lax.dot_general(q.astype(BF), k.astype(BF), (((1,), (1,)), ((), ())),
                             preferred_element_type=jnp.float32) * intra
    s_old = s_ref[...]
    o = (jnp.dot(scores.astype(BF), v, preferred_element_type=jnp.float32)
         + jnp.dot((q * q_dec).astype(BF), s_old.astype(BF), preferred_element_type=jnp.float32))
    s_ref[...] = c_dec * s_old + lax.dot_general((k * k_dec).astype(BF), v, (((0,), (0,)), ((), ())),
                                                 preferred_element_type=jnp.float32)
    if reverse:
        o = o + of_ref[0]
        mu = jnp.mean(o, axis=-1, keepdims=True)
        oc = o - mu
        var = jnp.mean(oc * oc, axis=-1, keepdims=True)
        g = gate_ref[0].astype(jnp.float32)
        y_ref[0] = (oc * lax.rsqrt(var + LN_EPS) * (g * jax.nn.sigmoid(g))).astype(y_ref.dtype)
    else:
        o_ref[0] = o


def retention(z, cos, sin, log_dec_f, log_dec_b, *, n_heads, qk_dim, v_dim, n_ctx,
              q_col, k_col, v_col, g_col, out_dtype=BF):
    B, ltot, _ = z.shape
    C = RET_TILE
    assert n_ctx % C == 0 and ltot % C == 0
    n_c = ltot // C
    n_cc = n_ctx // C
    k_scale = qk_dim ** -0.5

    def fwd_idx(c):
        return c

    def bwd_idx(c):
        return jnp.where(c < n_cc, n_cc - 1 - c, n_c - 1 - (c - n_cc))

    def specs(cidx):
        return [
            pl.BlockSpec((1, C, qk_dim), lambda b, h, c, lg: (b, cidx(c), q_col + h)),
            pl.BlockSpec((1, C, qk_dim), lambda b, h, c, lg: (b, cidx(c), k_col + h)),
            pl.BlockSpec((1, C, v_dim), lambda b, h, c, lg: (b, cidx(c), v_col + h)),
            pl.BlockSpec((C, qk_dim), lambda b, h, c, lg: (cidx(c), 0)),
            pl.BlockSpec((C, qk_dim), lambda b, h, c, lg: (cidx(c), 0)),
        ]

    o_f = pl.pallas_call(
        functools.partial(_ret_body, reverse=False, k_scale=k_scale),
        grid_spec=pltpu.PrefetchScalarGridSpec(
            num_scalar_prefetch=1, grid=(B, n_heads, n_c),
            in_specs=specs(fwd_idx),
            out_specs=pl.BlockSpec((1, C, v_dim), lambda b, h, c, lg: (b, c, h)),
            scratch_shapes=[pltpu.VMEM((qk_dim, v_dim), jnp.float32)]),
        out_shape=jax.ShapeDtypeStruct((B, ltot, n_heads * v_dim), jnp.float32),
        compiler_params=_cparams(3), name="retention_fwd",
    )(log_dec_f.astype(jnp.float32), z, z, z, cos, sin)

    return pl.pallas_call(
        functools.partial(_ret_body, reverse=True, k_scale=k_scale),
        grid_spec=pltpu.PrefetchScalarGridSpec(
            num_scalar_prefetch=1, grid=(B, n_heads, n_c),
            in_specs=specs(bwd_idx) + [
                pl.BlockSpec((1, C, v_dim), lambda b, h, c, lg: (b, bwd_idx(c), h)),
                pl.BlockSpec((1, C, v_dim), lambda b, h, c, lg: (b, bwd_idx(c), g_col + h)),
            ],
            out_specs=pl.BlockSpec((1, C, v_dim), lambda b, h, c, lg: (b, bwd_idx(c), h)),
            scratch_shapes=[pltpu.VMEM((qk_dim, v_dim), jnp.float32)]),
        out_shape=jax.ShapeDtypeStruct((B, ltot, n_heads * v_dim), out_dtype),
        compiler_params=_cparams(3), name="retention_bwd",
    )(log_dec_b.astype(jnp.float32), z, z, z, cos, sin, o_f, z)


RET_CHUNK = 128


def _head_norm(x):
    mu = x.mean(-1, keepdims=True)
    var = jnp.square(x - mu).mean(-1, keepdims=True)
    return (x - mu) * lax.rsqrt(var + LN_EPS)


def _axial_rope(x, pos_row, pos_col):
    half = x.shape[-1] // 2
    n_freq = half // 2
    freq = ROPE_BASE ** (-jnp.arange(n_freq, dtype=jnp.float32) / n_freq)

    def rotate(xp, pos):
        ang = pos.astype(jnp.float32)[:, None] * freq[None, :]
        cos = jnp.cos(ang)[None, :, None, :]
        sin = jnp.sin(ang)[None, :, None, :]
        x1, x2 = xp[..., :n_freq], xp[..., n_freq:]
        return jnp.concatenate([x1 * cos - x2 * sin, x1 * sin + x2 * cos], axis=-1)

    return jnp.concatenate([rotate(x[..., :half], pos_row), rotate(x[..., half:], pos_col)], axis=-1)


def _retention_chunkwise(q, k, v, log_gamma, s0, with_output):
    B, L, H, _ = q.shape
    dv = v.shape[-1]
    n = L // RET_CHUNK
    idx = jnp.arange(RET_CHUNK, dtype=jnp.float32)
    rel = idx[:, None] - idx[None, :]
    intra = jnp.where(rel >= 0, jnp.exp(log_gamma[:, None, None] * jnp.maximum(rel, 0.0)), 0.0)
    q_dec = jnp.exp(log_gamma[None, :] * (idx[:, None] + 1.0))[None, :, :, None]
    k_dec = jnp.exp(log_gamma[None, :] * (RET_CHUNK - 1.0 - idx[:, None]))[None, :, :, None]
    c_dec = jnp.exp(log_gamma * RET_CHUNK)[None, :, None, None]

    def blocks(z):
        return z.reshape(B, n, RET_CHUNK, H, z.shape[-1]).swapaxes(0, 1)

    def step(s, qkv):
        qi, ki, vi = qkv
        s_new = c_dec * s + jnp.einsum('bchd,bche->bhde', ki * k_dec, vi)
        if not with_output:
            return s_new, None
        scores = jnp.einsum('bqhd,bkhd->bhqk', qi, ki) * intra[None]
        o = (jnp.einsum('bhqk,bkhe->bqhe', scores, vi)
             + jnp.einsum('bqhd,bhde->bqhe', qi * q_dec, s))
        return s_new, o

    s_fin, o = lax.scan(step, s0, (blocks(q), blocks(k), blocks(v)))
    if with_output:
        o = o.swapaxes(0, 1).reshape(B, L, H, dv)
    return o, s_fin


def _bidirectional_retention(q, k, v, qc, kc, vc, log_dec_f, log_dec_b):
    s0 = jnp.zeros((q.shape[0], RET_HEADS, RET_QK_DIM, RET_V_DIM), jnp.float32)
    lf = log_dec_f.astype(jnp.float32)
    lb = log_dec_b.astype(jnp.float32)

    def flip(t):
        return t[:, ::-1]

    oc_f, s_cf = _retention_chunkwise(qc, kc, vc, lf, s0, True)
    o_f, _ = _retention_chunkwise(q, k, v, lf, s_cf, True)
    oc_b, s_cb = _retention_chunkwise(flip(qc), flip(kc), flip(vc), lb, s0, True)
    o_b, _ = _retention_chunkwise(flip(q), flip(k), flip(v), lb, s_cb, True)
    return o_f + flip(o_b), oc_f + flip(oc_b)


def retention_jax(z, log_dec_f, log_dec_b, n_ctx):
    B, ltot, _ = z.shape
    L = ltot - n_ctx
    zf = z.astype(jnp.float32)
    base = 3 * NA_WIDTH
    sp = [base, base + RET_QK_WIDTH, base + 2 * RET_QK_WIDTH, base + 2 * RET_QK_WIDTH + RET_V_WIDTH,
          base + 2 * RET_QK_WIDTH + 2 * RET_V_WIDTH]
    zl = [zf[:, n_ctx:, sp[i]:sp[i + 1]] for i in range(4)]
    zc = [zf[:, :n_ctx, sp[i]:sp[i + 1]] for i in range(4)]

    def heads(t):
        return t.reshape(B, t.shape[1], RET_HEADS, -1)

    t = jnp.arange(L)
    pos_row = t // GRID_W
    pos_col = t % GRID_W
    k_scale = RET_QK_DIM ** -0.5
    qb = _axial_rope(heads(zl[0]), pos_row, pos_col)
    kb = _axial_rope(heads(zl[1]), pos_row, pos_col) * k_scale
    o, oc = _bidirectional_retention(qb, kb, heads(zl[2]), heads(zc[0]), heads(zc[1]) * k_scale, heads(zc[2]),
                                     log_dec_f, log_dec_b)

    def merge(o_b, gate):
        return _head_norm(o_b).reshape(B, o_b.shape[1], RET_V_WIDTH) * jax.nn.silu(gate)

    return jnp.concatenate([merge(oc, zc[3]), merge(o, zl[3])], axis=1).astype(BF)


def _s5_discretize(lam_re, lam_im, log_dt, b_re, b_im):
    dt = jnp.exp(log_dt)[:, None]
    logmag = lam_re * dt
    angle = lam_im * dt
    nr = jnp.exp(logmag) * jnp.cos(angle) - 1.0
    ni = jnp.exp(logmag) * jnp.sin(angle)
    den = lam_re * lam_re + lam_im * lam_im
    fr = ((nr * lam_re + ni * lam_im) / den)[..., None]
    fi = ((ni * lam_re - nr * lam_im) / den)[..., None]
    return logmag, angle, fr * b_re - fi * b_im, fr * b_im + fi * b_re


def s5_operators(lam_re, lam_im, log_dt, b_re, b_im, c_re, c_im):
    T, CH, P = S5_T, S5_GROUP_CH, S5_STATE
    W = T * CH
    G = lam_re.shape[1]
    tau = jnp.arange(T + 1, dtype=jnp.float32)[:, None, None]
    zeros_w = jnp.zeros((G, CH, W), jnp.float32)
    kt_t = 0.0
    wbs, wcs, a_rows = [], [], []
    for d in range(2):
        logmag, angle, bb_re, bb_im = _s5_discretize(lam_re[d], lam_im[d], log_dt[d], b_re, b_im)
        mag = jnp.exp(logmag[None] * tau)
        pr = mag * jnp.cos(angle[None] * tau)
        pi = mag * jnp.sin(angle[None] * tau)
        cr, ci = c_re[d], c_im[d]
        bt_re = bb_re.transpose(0, 2, 1)
        bt_im = bb_im.transpose(0, 2, 1)
        z_re = pr[:, :, None, :] * bt_re[None] - pi[:, :, None, :] * bt_im[None]
        z_im = pr[:, :, None, :] * bt_im[None] + pi[:, :, None, :] * bt_re[None]
        zl_re = z_re[:T][::-1] if d == 0 else z_re[:T]
        zl_im = z_im[:T][::-1] if d == 0 else z_im[:T]
        m = (jnp.einsum('ghp,xgkp->ghxk', cr, zl_re, precision=HI)
             - jnp.einsum('ghp,xgkp->ghxk', ci, zl_im, precision=HI)).reshape(G, CH, W)
        padded = jnp.concatenate([m, zeros_w] if d == 0 else [zeros_w, m], axis=-1)
        rows = [padded[:, :, CH * (T - 1 - t):CH * (T - 1 - t) + W] if d == 0
                else padded[:, :, W - CH * t:2 * W - CH * t] for t in range(T)]
        kt_t = kt_t + jnp.stack(rows, axis=1).reshape(G, W, W)
        wbs += [zl_re.transpose(1, 0, 2, 3).reshape(G, W, P),
                zl_im.transpose(1, 0, 2, 3).reshape(G, W, P)]
        qr = (pr[1:] if d == 0 else pr[1:][::-1])[:, :, None, :]
        qi = (pi[1:] if d == 0 else pi[1:][::-1])[:, :, None, :]
        wcs += [(cr[None] * qr - ci[None] * qi).transpose(1, 0, 2, 3).reshape(G, W, P),
                (-(cr[None] * qi + ci[None] * qr)).transpose(1, 0, 2, 3).reshape(G, W, P)]
        a_rows += [pr[T].reshape(-1), pi[T].reshape(-1)]

    def pair_diag(w):
        g2 = w.reshape(G // 2, 2, w.shape[1], w.shape[2])
        z = jnp.zeros_like(g2[:, 0])
        return jnp.concatenate([jnp.concatenate([g2[:, 0], z], axis=2),
                                jnp.concatenate([z, g2[:, 1]], axis=2)], axis=1)

    wb = jnp.concatenate([pair_diag(w) for w in wbs], axis=2)
    wc_t = jnp.concatenate([pair_diag(w) for w in wcs], axis=2)
    return kt_t.astype(BF), wb.astype(BF), wc_t.astype(BF), jnp.stack(a_rows)


def _s5_body(u_ref, kt_ref, wb_ref, wc_ref, a_ref, d_ref, y_ref, e_ref, x_ref, ug_ref, yg_ref, us_ref, *, n_c, n_cc):
    T, CH, P2, W, GB = S5_T, S5_GROUP_CH, 2 * S5_STATE, S5_T * S5_GROUP_CH, S5_TILE_GROUPS
    n_pair = GB // 2
    half = LANES // CH
    lane = lax.broadcasted_iota(jnp.int32, (n_c, LANES), 1)

    for s in range(T):
        us_ref[s] = u_ref[0, pl.ds(s, n_c, stride=T), :]

    def u_step(s):
        return us_ref[s]

    def place(x, src_slot, dst_slot):
        shift = (CH * (dst_slot - src_slot)) % LANES
        r = pltpu.roll(x, shift, 1) if shift else x
        return jnp.where((lane >= CH * dst_slot) & (lane < CH * (dst_slot + 1)), r, 0.0)

    for g in range(GB):
        for k in range(T // half):
            acc = place(u_step(half * k), g, 0)
            for s8 in range(1, half):
                acc = acc + place(u_step(half * k + s8), g, s8)
            ug_ref[:, W * g + LANES * k:W * g + LANES * (k + 1)] = acc.astype(BF)
    for q in range(n_pair):
        e = jnp.dot(ug_ref[:, 2 * W * q:2 * W * (q + 1)], wb_ref[q], preferred_element_type=jnp.float32)
        for comp in range(4):
            e_ref[comp, :, q * P2:(q + 1) * P2] = e[:, comp * P2:(comp + 1) * P2]
    a_fr, a_fi, a_br, a_bi = a_ref[0:1, :], a_ref[1:2, :], a_ref[2:3, :], a_ref[3:4, :]

    def step(i, carry):
        xfr, xfi, xbr, xbi = carry
        j = jnp.where(i < n_cc, n_cc - 1 - i, n_c - 1 - (i - n_cc))
        x_ref[0, pl.ds(i, 1), :] = xfr
        x_ref[1, pl.ds(i, 1), :] = xfi
        x_ref[2, pl.ds(j, 1), :] = xbr
        x_ref[3, pl.ds(j, 1), :] = xbi
        return (a_fr * xfr - a_fi * xfi + e_ref[0, pl.ds(i, 1), :],
                a_fr * xfi + a_fi * xfr + e_ref[1, pl.ds(i, 1), :],
                a_br * xbr - a_bi * xbi + e_ref[2, pl.ds(j, 1), :],
                a_br * xbi + a_bi * xbr + e_ref[3, pl.ds(j, 1), :])

    zero = jnp.zeros((1, n_pair * P2), jnp.float32)
    lax.fori_loop(0, n_c, step, (zero, zero, zero, zero))
    nt_dims = (((1,), (1,)), ((), ()))
    for q in range(n_pair):
        xcat = jnp.concatenate([x_ref[comp, :, q * P2:(q + 1) * P2] for comp in range(4)], axis=1).astype(BF)
        yx = lax.dot_general(xcat, wc_ref[q], nt_dims, preferred_element_type=jnp.float32)
        for gi in range(2):
            g = 2 * q + gi
            yg_ref[:, W * g:W * (g + 1)] = (
                lax.dot_general(ug_ref[:, W * g:W * (g + 1)], kt_ref[g], nt_dims, preferred_element_type=jnp.float32)
                + yx[:, gi * W:(gi + 1) * W])
    d = d_ref[...]
    for s in range(T):
        k, s8 = divmod(s, half)
        acc = place(yg_ref[:, LANES * k:LANES * (k + 1)], s8, 0)
        for g in range(1, GB):
            acc = acc + place(yg_ref[:, W * g + LANES * k:W * g + LANES * (k + 1)], s8, g)
        y_ref[0, pl.ds(s, n_c, stride=T), :] = jax.nn.gelu(acc + d * u_step(s))


def s5_scan_gelu(u, ops, d_skip, *, n_ctx):
    kt, wb, wc, a = ops
    B, ltot, D = u.shape
    T, CH, P, GB = S5_T, S5_GROUP_CH, S5_STATE, S5_TILE_GROUPS
    W = T * CH
    n_c = ltot // T
    body = functools.partial(_s5_body, n_c=n_c, n_cc=n_ctx // T)
    tile = pl.BlockSpec((1, ltot, LANES), lambda j, b: (b, 0, j))
    return pl.pallas_call(
        body,
        grid=(D // LANES, B),
        in_specs=[
            tile,
            pl.BlockSpec((GB, W, W), lambda j, b: (j, 0, 0)),
            pl.BlockSpec((GB // 2, 2 * W, 8 * P), lambda j, b: (j, 0, 0)),
            pl.BlockSpec((GB // 2, 8 * P, 2 * W), lambda j, b: (j, 0, 0)),
            pl.BlockSpec((4, GB * P), lambda j, b: (0, j)),
            pl.BlockSpec((1, LANES), lambda j, b: (0, j)),
        ],
        out_specs=tile,
        out_shape=jax.ShapeDtypeStruct((B, ltot, D), jnp.float32),
        scratch_shapes=[pltpu.VMEM((4, n_c, GB * P), jnp.float32),
                        pltpu.VMEM((4, n_c, GB * P), jnp.float32),
                        pltpu.VMEM((n_c, GB * W), BF),
                        pltpu.VMEM((n_c, GB * W), jnp.float32),
                        pltpu.VMEM((T, n_c, LANES), jnp.float32)],
        compiler_params=_cparams(2, S5_VMEM_LIMIT_BYTES),
        name="s5_scan",
    )(u, kt, wb, wc, a, d_skip.reshape(1, D))


ROUTER_LANES = 128


def _router_body(h_ref, w_ref, b_ref, o_ref):
    logits = jnp.dot(h_ref[...], w_ref[...], preferred_element_type=jnp.float32) + b_ref[...]
    lane = lax.broadcasted_iota(jnp.int32, logits.shape, 1)

    def first_argmax(vals, vmax):
        return jnp.min(jnp.where(vals == vmax, lane, ROUTER_LANES), axis=-1, keepdims=True)

    gl = jnp.where(lane < MOE_GROUPS, logits, NEG_BIG)
    gmax = jnp.max(gl, axis=-1, keepdims=True)
    gsum = jnp.sum(jnp.where(lane < MOE_GROUPS, jnp.exp(gl - gmax), 0.0), axis=-1, keepdims=True)
    g_w = 1.0 / gsum
    lo = MOE_GROUPS + MOE_EXPERTS_PER_GROUP * first_argmax(gl, gmax)
    el = jnp.where((lane >= lo) & (lane < lo + MOE_EXPERTS_PER_GROUP), logits, NEG_BIG)
    m1 = jnp.max(el, axis=-1, keepdims=True)
    i1 = first_argmax(el, m1)
    el2 = jnp.where(lane == i1, NEG_BIG, el)
    m2 = jnp.max(el2, axis=-1, keepdims=True)
    i2 = first_argmax(el2, m2)
    e2 = jnp.exp(m2 - m1)
    den = 1.0 + e2
    o_ref[...] = jnp.where(lane == 0, (i1 - MOE_GROUPS).astype(jnp.float32),
                           jnp.where(lane == 1, (i2 - MOE_GROUPS).astype(jnp.float32),
                                     jnp.where(lane == 2, g_w * (1.0 / den),
                                               jnp.where(lane == 3, g_w * (e2 / den), 0.0))))


def moe_router(h, w_group, b_group, w_expert, b_expert):
    n, d = h.shape
    n_log = MOE_GROUPS + MOE_EXPERTS
    w = jnp.zeros((d, ROUTER_LANES), jnp.float32).at[:, :MOE_GROUPS].set(w_group).at[:, MOE_GROUPS:n_log].set(w_expert)
    b = jnp.zeros((1, ROUTER_LANES), jnp.float32).at[0, :MOE_GROUPS].set(b_group).at[0, MOE_GROUPS:n_log].set(b_expert)
    tm = MM_TILE_M if n % MM_TILE_M == 0 else n
    route = pl.pallas_call(
        _router_body,
        grid=(n // tm,),
        in_specs=[pl.BlockSpec((tm, d), lambda i: (i, 0)),
                  pl.BlockSpec((d, ROUTER_LANES), lambda i: (0, 0)),
                  pl.BlockSpec((1, ROUTER_LANES), lambda i: (0, 0))],
        out_specs=pl.BlockSpec((tm, ROUTER_LANES), lambda i: (i, 0)),
        out_shape=jax.ShapeDtypeStruct((n, ROUTER_LANES), jnp.float32),
        compiler_params=_cparams(1),
        name="moe_router",
    )(h, w.astype(BF), b)
    return route[:, 0:2].astype(jnp.int32), route[:, 2:4]


def _expert_body(blk_e_ref, blk_on_ref, x_ref, wg_ref, wu_ref, wd_ref, y_ref):
    i = pl.program_id(0)

    @pl.when(blk_on_ref[i] != 0)
    def _compute():
        x = x_ref[...]
        g = jnp.dot(x, wg_ref[0], preferred_element_type=jnp.float32)
        u = jnp.dot(x, wu_ref[0], preferred_element_type=jnp.float32)
        hid = (g * jax.nn.sigmoid(g) * u).astype(BF)
        y_ref[...] = jnp.dot(hid, wd_ref[0], preferred_element_type=jnp.float32).astype(y_ref.dtype)

    @pl.when(blk_on_ref[i] == 0)
    def _unused_block():
        y_ref[...] = jnp.zeros_like(y_ref)


def expert_mlp(xb, blk_e, blk_on, w_gate, w_up, w_down, out_dtype=jnp.float32):
    n_rows, d = xb.shape
    hid = w_gate.shape[2]
    return pl.pallas_call(
        _expert_body,
        grid_spec=pltpu.PrefetchScalarGridSpec(
            num_scalar_prefetch=2, grid=(n_rows // EXPERT_BLOCK,),
            in_specs=[pl.BlockSpec((EXPERT_BLOCK, d), lambda i, be, on: (i, 0)),
                      pl.BlockSpec((1, d, hid), lambda i, be, on: (be[i], 0, 0)),
                      pl.BlockSpec((1, d, hid), lambda i, be, on: (be[i], 0, 0)),
                      pl.BlockSpec((1, hid, d), lambda i, be, on: (be[i], 0, 0))],
            out_specs=pl.BlockSpec((EXPERT_BLOCK, d), lambda i, be, on: (i, 0))),
        out_shape=jax.ShapeDtypeStruct((n_rows, d), out_dtype),
        compiler_params=_cparams(1),
        name="expert_mlp",
    )(blk_e, blk_on, xb, w_gate, w_up, w_down)


def moe_dispatch(eid, n_tok):
    n_asg = n_tok * MOE_TOP_K
    flat_e = eid.reshape(-1)
    order = jnp.argsort(flat_e)
    e_sorted = flat_e[order]
    counts = jnp.sum(flat_e[:, None] == jnp.arange(MOE_EXPERTS)[None, :], axis=0).astype(jnp.int32)
    padded = (counts + EXPERT_BLOCK - 1) // EXPERT_BLOCK * EXPERT_BLOCK
    pad_end = jnp.cumsum(padded)
    pad_start = pad_end - padded
    start = jnp.cumsum(counts) - counts
    dest_sorted = pad_start[e_sorted] + jnp.arange(n_asg, dtype=jnp.int32) - start[e_sorted]
    n_blocks = (n_asg + MOE_EXPERTS * (EXPERT_BLOCK - 1)) // EXPERT_BLOCK + 1
    n_rows = n_blocks * EXPERT_BLOCK
    blk_first = jnp.arange(n_blocks, dtype=jnp.int32) * EXPERT_BLOCK
    blk_e = jnp.minimum(jnp.sum(blk_first[:, None] >= pad_end[None, :], axis=1), MOE_EXPERTS - 1).astype(jnp.int32)
    blk_on = (blk_first < pad_end[-1]).astype(jnp.int32)
    row = jnp.arange(n_rows, dtype=jnp.int32)
    row_e = jnp.repeat(blk_e, EXPERT_BLOCK)
    pos = row - pad_start[row_e]
    valid = (pos < counts[row_e]) & (row < pad_end[-1])
    src = jnp.clip(start[row_e] + pos, 0, n_asg - 1)
    row_tok = jnp.where(valid, order[src] // MOE_TOP_K, n_tok).astype(jnp.int32)
    dest = dest_sorted[jnp.argsort(order)].reshape(n_tok, MOE_TOP_K)
    return row_tok, dest, blk_e, blk_on


def hier_moe(h, w_group, b_group, w_expert, b_expert, w_gate, w_up, w_down):
    n_tok, d = h.shape
    eid, wts = moe_router(h, w_group, b_group, w_expert, b_expert)
    row_tok, dest, blk_e, blk_on = moe_dispatch(eid, n_tok)
    h_pad = jnp.concatenate([h, jnp.zeros((1, d), h.dtype)], axis=0)
    yb = expert_mlp(h_pad[row_tok], blk_e, blk_on, w_gate.astype(BF), w_up.astype(BF), w_down.astype(BF), BF)
    return (yb[dest[:, 0]].astype(jnp.float32) * wts[:, 0:1]
            + yb[dest[:, 1]].astype(jnp.float32) * wts[:, 1:2])


def kernel(x, c, ctx, c_ctx, mod_w, mod_b, ln_g, ln_b, mix_w_in, mix_w_out, na_rpb,
           ret_log_decay_fwd, ret_log_decay_bwd, s5_w_in, s5_lam_re, s5_lam_im, s5_log_dt,
           s5_b_re, s5_b_im, s5_c_re, s5_c_im, s5_d, s5_w_glu, s5_w_out,
           moe_w_group, moe_b_group, moe_w_expert, moe_b_expert, moe_w_gate, moe_w_up, moe_w_down):
    B, L, D = x.shape
    n_ctx = ctx.shape[1]
    ltot = n_ctx + L
    n_tok = B * ltot
    xa = jnp.concatenate([ctx, x], axis=1)
    cond = jnp.concatenate([jax.nn.silu(c), jax.nn.silu(c_ctx)[None]], axis=0)
    mods = []
    for layer in range(DEPTH):
        m = (cond @ mod_w[layer] + mod_b[layer]).reshape(B + 1, 6, 1, D)
        mods.append([m[:, i] for i in range(6)])
    cos, sin = rope_tables(n_ctx, L, RET_QK_DIM)

    h = modulate(xa, mods[0][0], mods[0][1], n_ctx=n_ctx)
    for layer in range(DEPTH):
        last = layer == DEPTH - 1
        m = mods[layer]
        j = layer // 2
        h2d = h.reshape(n_tok, D)
        if layer % 2 == 0:
            z = _mm(h2d, mix_w_in[j].astype(BF), BF).reshape(B, ltot, -1)
            o_na = na_attention(z, na_bias_table(na_rpb[j]), n_heads=NA_HEADS, head_dim=NA_HEAD_DIM,
                                n_ctx=n_ctx, q_col=0, k_col=NA_HEADS, v_col=2 * NA_HEADS)
            y_ret = retention_jax(z, ret_log_decay_fwd[j], ret_log_decay_bwd[j], n_ctx)
            y = jnp.concatenate([o_na, y_ret], axis=-1).reshape(n_tok, D)
            o = _mm(y, mix_w_out[j].astype(BF))
        else:
            u = _mm(h2d, s5_w_in[j].astype(BF)).reshape(B, ltot, D)
            ops = s5_operators(s5_lam_re[j], s5_lam_im[j], s5_log_dt[j], s5_b_re[j], s5_b_im[j],
                               s5_c_re[j], s5_c_im[j])
            g = s5_scan_gelu(u, ops, s5_d[j], n_ctx=n_ctx).astype(BF).reshape(n_tok, D)
            gg = _mm(g, s5_w_glu[j].astype(BF), BF, glu_gate=g)
            o = _mm(gg, s5_w_out[j].astype(BF))
        x1, h_moe = residual_ln(xa, o.reshape(B, ltot, D), m[2], ln_g[layer, 0], ln_b[layer, 0],
                                n_ctx=n_ctx, next_shift=m[3], next_scale=m[4])
        f = hier_moe(h_moe.reshape(n_tok, D),
                     moe_w_group[layer], moe_b_group[layer], moe_w_expert[layer], moe_b_expert[layer],
                     moe_w_gate[layer], moe_w_up[layer], moe_w_down[layer]).reshape(B, ltot, D)
        if last:
            return residual_ln(x1, f, m[5], ln_g[layer, 1], ln_b[layer, 1], n_ctx=n_ctx, latent_only=True)
        xa, h = residual_ln(x1, f, m[5], ln_g[layer, 1], ln_b[layer, 1], n_ctx=n_ctx,
                            next_shift=mods[layer + 1][0], next_scale=mods[layer + 1][1])
```
